```python
import math
import jax, jax.numpy as jnp
from jax import lax
import numpy as np

D_MODEL = 1024
BATCH = 1
SEQ = 16384
DEPTH = 1
DEC_BATCH = 2
DEC_SEQ = 8192
PAST_LEN = 128

N_META = 16
CHUNK = 128
LEAD_PAD = CHUNK - N_META
EPS = 1e-6
MLA_HEADS = 8
Q_LORA = 256
KV_LORA = 128
QK_NOPE = 64
QK_ROPE = 32
V_HEAD = 64
QK_DIM = QK_NOPE + QK_ROPE
MLA_WIDTH = MLA_HEADS * V_HEAD
ROPE_THETA = 10000.0
M_HEADS = 4
M_HEAD_DIM = 128
M_WIDTH = M_HEADS * M_HEAD_DIM
CONV_W = 3
D_FF = -((-8 * D_MODEL) // (3 * 256)) * 256
SEG = (Q_LORA, KV_LORA, QK_ROPE, M_WIDTH, M_WIDTH, M_WIDTH, M_WIDTH, 4 * M_HEADS, D_MODEL, D_MODEL)
SPLIT_IDX = tuple(sum(SEG[:i + 1]) for i in range(len(SEG) - 1))
D_IN = sum(SEG)
OFF_MGATES = sum(SEG[:7])

kernel_name = "hybrid_mla_mlstm_encoder"


def rmsnorm(x, g):
    xf = x.astype(jnp.float32)
    y = xf * lax.rsqrt(jnp.mean(xf * xf, axis=-1, keepdims=True) + EPS)
    return (y * g.astype(jnp.float32)).astype(x.dtype)


def rope(x):
    T = x.shape[1]
    half = QK_ROPE // 2
    freqs = ROPE_THETA ** (-jnp.arange(half, dtype=jnp.float32) / half)
    ang = jnp.arange(T, dtype=jnp.float32)[:, None] * freqs[None, :]
    shape = (1, T) + (1,) * (x.ndim - 3) + (half,)
    cos = jnp.cos(ang).reshape(shape)
    sin = jnp.sin(ang).reshape(shape)
    xf = x.astype(jnp.float32)
    x1, x2 = xf[..., :half], xf[..., half:]
    return jnp.concatenate([x1 * cos - x2 * sin, x2 * cos + x1 * sin], axis=-1).astype(x.dtype)


def mla_attention(c_q, c_kv, k_r, q_norm_g, kv_norm_g, w_uq, w_ukv):
    B, T, _ = c_q.shape
    q = (rmsnorm(c_q, q_norm_g) @ w_uq).reshape(B, T, MLA_HEADS, QK_DIM)
    q = jnp.concatenate([q[..., :QK_NOPE], rope(q[..., QK_NOPE:])], axis=-1) * (QK_DIM ** -0.5)
    kv = (rmsnorm(c_kv, kv_norm_g) @ w_ukv).reshape(B, T, MLA_HEADS, QK_NOPE + V_HEAD)
    k_nope, v = kv[..., :QK_NOPE], kv[..., QK_NOPE:]
    k_rope = jnp.broadcast_to(rope(k_r)[:, :, None, :], (B, T, MLA_HEADS, QK_ROPE))
    k = jnp.concatenate([k_nope, k_rope], axis=-1)
    q = jnp.pad(q, ((0, 0), (LEAD_PAD, 0), (0, 0), (0, 0)))
    nb = (T + LEAD_PAD) // CHUNK
    qb = q.reshape(B, nb, CHUNK, MLA_HEADS, QK_DIM).transpose(1, 0, 2, 3, 4)

    def block(qi):
        s = jnp.einsum('bqhd,bkhd->bhqk', qi, k, preferred_element_type=jnp.float32)
        p = jax.nn.softmax(s, axis=-1).astype(v.dtype)
        return jnp.einsum('bhqk,bkhd->bqhd', p, v)

    o = lax.map(block, qb)
    o = o.transpose(1, 0, 2, 3, 4).reshape(B, T + LEAD_PAD, MLA_WIDTH)
    return o[:, LEAD_PAD:]


def mlstm_chunkwise(q, k, v, a, b):
    F = jnp.cumsum(a, axis=-1)
    FL = F[..., -1]
    g = FL[..., None] - F + b

    def step(carry, xs):
        C, n, m = carry
        k_c, v_c, g_c, FL_c = xs
        m_new = jnp.maximum(FL_c + m, jnp.max(g_c, axis=-1))
        decay = jnp.exp(FL_c + m - m_new)
        w = jnp.exp(g_c - m_new[..., None])
        C_new = decay[..., None, None] * C + jnp.einsum('bhs,bhsd,bhse->bhde', w, v_c, k_c)
        n_new = decay[..., None] * n + jnp.einsum('bhs,bhse->bhe', w, k_c)
        return (C_new, n_new, m_new), (C, n, m)

    B, NH, NC, L, dh = q.shape
    init = (jnp.zeros((B, NH, dh, dh), jnp.float32), jnp.zeros((B, NH, dh), jnp.float32),
            jnp.zeros((B, NH), jnp.float32))
    xs = (k.transpose(2, 0, 1, 3, 4), v.transpose(2, 0, 1, 3, 4), g.transpose(2, 0, 1, 3), FL.transpose(2, 0, 1))
    _, (Cs, ns, ms) = lax.scan(step, init, xs)
    Cs = Cs.transpose(1, 2, 0, 3, 4)
    ns = ns.transpose(1, 2, 0, 3)
    ms = ms.transpose(1, 2, 0)
    causal = jnp.tril(jnp.ones((L, L), dtype=bool))
    logD = jnp.where(causal, F[..., :, None] - F[..., None, :] + b[..., None, :], -jnp.inf)
    inter = F + ms[..., None]
    m = jnp.maximum(inter, jnp.max(logD, axis=-1))
    S = jnp.einsum('bhcjd,bhcsd->bhcjs', q, k) * jnp.exp(logD - m[..., None])
    wi = jnp.exp(inter - m)
    num = wi[..., None] * jnp.einsum('bhcde,bhcje->bhcjd', Cs, q) + jnp.einsum('bhcjs,bhcsd->bhcjd', S, v)
    den = wi * jnp.einsum('bhce,bhcje->bhcj', ns, q) + jnp.sum(S, axis=-1)
    return num / jnp.maximum(jnp.abs(den), jnp.exp(-m))[..., None]


def mlstm_branch(mq, mk, mv, mo, mgates, conv_w, conv_b, m_norm_g):
    B, T, _ = mq.shape
    dtype = mq.dtype
    qk = jnp.concatenate([mq, mk], axis=-1)
    qk = lax.conv_general_dilated(qk, conv_w[:, None, :].astype(qk.dtype), window_strides=(1,),
                                  padding=[(CONV_W // 2, CONV_W // 2)],
                                  dimension_numbers=('NWC', 'WIO', 'NWC'),
                                  feature_group_count=2 * M_WIDTH) + conv_b
    qk = jax.nn.silu(qk).astype(jnp.float32)
    q = qk[..., :M_WIDTH]
    k = qk[..., M_WIDTH:] * (M_HEAD_DIM ** -0.5)
    v = mv.astype(jnp.float32)
    gts = mgates.astype(jnp.float32)
    i_f, f_f, i_b, f_b = (gts[..., j * M_HEADS:(j + 1) * M_HEADS] for j in range(4))
    Tp = T + LEAD_PAD
    NC = Tp // CHUNK
    padt = lambda x: jnp.pad(x, ((0, 0), (LEAD_PAD, 0)) + ((0, 0),) * (x.ndim - 2))
    q, k, v = (padt(t).reshape(B, Tp, M_HEADS, M_HEAD_DIM) for t in (q, k, v))
    i_f, f_f, i_b, f_b = (padt(t) for t in (i_f, f_f, i_b, f_b))
    valid = (jnp.arange(Tp) >= LEAD_PAD)[None, :, None]

    def run(q, k, v, ig, fg, valid):
        a = jnp.where(valid, jax.nn.log_sigmoid(fg), 0.0)
        b = jnp.where(valid, ig, -jnp.inf)
        ch = lambda x: x.reshape(B, NC, CHUNK, M_HEADS, M_HEAD_DIM).transpose(0, 3, 1, 2, 4)
        cg = lambda x: x.reshape(B, NC, CHUNK, M_HEADS).transpose(0, 3, 1, 2)
        h = mlstm_chunkwise(ch(q), ch(k), ch(v), cg(a), cg(b))
        return h.transpose(0, 2, 3, 1, 4).reshape(B, Tp, M_HEADS, M_HEAD_DIM)

    h_f = run(q, k, v, i_f, f_f, valid)
    fl = lambda x: jnp.flip(x, axis=1)
    h_b = fl(run(fl(q), fl(k), fl(v), fl(i_b), fl(f_b), fl(valid)))
    h = (h_f + h_b)[:, LEAD_PAD:]
    h = h * jax.nn.sigmoid(mo.astype(jnp.float32)).reshape(B, T, M_HEADS, M_HEAD_DIM)
    h = rmsnorm(h, m_norm_g.reshape(M_HEADS, M_HEAD_DIM))
    return h.reshape(B, T, M_WIDTH).astype(dtype)


def encoder_layer(x, norm1_g, w_in, b_in, conv_w, conv_b, q_norm_g, kv_norm_g, w_uq, w_ukv,
                  m_norm_g, w_pa, w_pb, w_o, norm2_g, w_ffn_gate, w_ffn_up, w_ffn_down):
    xn = rmsnorm(x, norm1_g)
    proj = xn @ w_in + b_in
    c_q, c_kv, k_r, mq, mk, mv, mo, mgates, g_a, g_b = jnp.split(proj, SPLIT_IDX, axis=-1)
    a_out = mla_attention(c_q, c_kv, k_r, q_norm_g, kv_norm_g, w_uq, w_ukv)
    m_out = mlstm_branch(mq, mk, mv, mo, mgates, conv_w, conv_b, m_norm_g)
    merged = jax.nn.sigmoid(g_a) * (a_out @ w_pa) + jax.nn.sigmoid(g_b) * (m_out @ w_pb)
    x = x + merged @ w_o
    xn2 = rmsnorm(x, norm2_g)
    return x + (jax.nn.silu(xn2 @ w_ffn_gate) * (xn2 @ w_ffn_up)) @ w_ffn_down


def trunk(x, meta_tokens, norm1_g, w_in, b_in, conv_w, conv_b, q_norm_g, kv_norm_g, w_uq, w_ukv,
          m_norm_g, w_pa, w_pb, w_o, norm2_g, w_ffn_gate, w_ffn_up, w_ffn_down, final_norm_g):
    B = x.shape[0]
    meta = jnp.broadcast_to(meta_tokens[None].astype(x.dtype), (B, N_META, D_MODEL))
    h = jnp.concatenate([meta, x], axis=1)
    for l in range(DEPTH):
        h = encoder_layer(h, norm1_g[l], w_in[l], b_in[l], conv_w[l], conv_b[l], q_norm_g[l], kv_norm_g[l],
                          w_uq[l], w_ukv[l], m_norm_g[l], w_pa[l], w_pb[l], w_o[l], norm2_g[l],
                          w_ffn_gate[l], w_ffn_up[l], w_ffn_down[l])
    h = rmsnorm(h, final_norm_g)
    return h[:, N_META:]


def setup_inputs(seed: int = 0) -> dict:
    key = jax.random.key(seed)
    ks = jax.random.split(key, 24)
    nrm = lambda k, shape, fan: jax.random.normal(k, shape, jnp.float32) * (fan ** -0.5)
    gain = lambda k, shape: 1.0 + 0.02 * jax.random.normal(k, shape, jnp.float32)
    b_in = 0.02 * jax.random.normal(ks[4], (DEPTH, D_IN), jnp.float32)
    f_bias = jax.random.uniform(ks[5], (DEPTH, 2, M_HEADS), jnp.float32, 3.0, 6.0)
    b_in = b_in.at[:, OFF_MGATES + M_HEADS:OFF_MGATES + 2 * M_HEADS].add(f_bias[:, 0])
    b_in = b_in.at[:, OFF_MGATES + 3 * M_HEADS:OFF_MGATES + 4 * M_HEADS].add(f_bias[:, 1])
    return {
        "x_prompt": jax.random.normal(ks[0], (BATCH, SEQ, D_MODEL), jnp.float32),
        "x_sample": jax.random.normal(ks[1], (DEC_BATCH, DEC_SEQ, D_MODEL), jnp.float32),
        "meta_tokens": jax.random.normal(ks[2], (N_META, D_MODEL), jnp.float32),
        "norm1_g": gain(ks[3], (DEPTH, D_MODEL)),
        "w_in": nrm(ks[6], (DEPTH, D_MODEL, D_IN), D_MODEL),
        "b_in": b_in,
        "conv_w": nrm(ks[7], (DEPTH, CONV_W, 2 * M_WIDTH), CONV_W),
        "conv_b": 0.02 * jax.random.normal(ks[8], (DEPTH, 2 * M_WIDTH), jnp.float32),
        "q_norm_g": gain(ks[9], (DEPTH, Q_LORA)),
        "kv_norm_g": gain(ks[10], (DEPTH, KV_LORA)),
        "w_uq": nrm(ks[11], (DEPTH, Q_LORA, MLA_HEADS * QK_DIM), Q_LORA),
        "w_ukv": nrm(ks[12], (DEPTH, KV_LORA, MLA_HEADS * (QK_NOPE + V_HEAD)), KV_LORA),
        "m_norm_g": gain(ks[13], (DEPTH, M_WIDTH)),
        "w_pa": nrm(ks[14], (DEPTH, MLA_WIDTH, D_MODEL), MLA_WIDTH),
        "w_pb": nrm(ks[15], (DEPTH, M_WIDTH, D_MODEL), M_WIDTH),
        "w_o": nrm(ks[16], (DEPTH, D_MODEL, D_MODEL), D_MODEL),
        "norm2_g": gain(ks[17], (DEPTH, D_MODEL)),
        "w_ffn_gate": nrm(ks[18], (DEPTH, D_MODEL, D_FF), D_MODEL),
        "w_ffn_up": nrm(ks[19], (DEPTH, D_MODEL, D_FF), D_MODEL),
        "w_ffn_down": nrm(ks[20], (DEPTH, D_FF, D_MODEL), D_FF),
        "final_norm_g": gain(ks[21], (D_MODEL,)),
    }


def reference(x_prompt, x_sample, meta_tokens, norm1_g, w_in, b_in, conv_w, conv_b, q_norm_g, kv_norm_g,
              w_uq, w_ukv, m_norm_g, w_pa, w_pb, w_o, norm2_g, w_ffn_gate, w_ffn_up, w_ffn_down, final_norm_g):
    y_prompt = trunk(x_prompt, meta_tokens, norm1_g, w_in, b_in, conv_w, conv_b, q_norm_g, kv_norm_g, w_uq,
                     w_ukv, m_norm_g, w_pa, w_pb, w_o, norm2_g, w_ffn_gate, w_ffn_up, w_ffn_down, final_norm_g)
    y_sample = trunk(x_sample, meta_tokens, norm1_g, w_in, b_in, conv_w, conv_b, q_norm_g, kv_norm_g, w_uq,
                     w_ukv, m_norm_g, w_pa, w_pb, w_o, norm2_g, w_ffn_gate, w_ffn_up, w_ffn_down, final_norm_g)
    return (y_prompt, y_sample)
```

```python
import functools

import jax
import jax.numpy as jnp
from jax import lax
from jax.experimental import pallas as pl
from jax.experimental.pallas import tpu as pltpu

F32 = jnp.float32
BF16 = jnp.bfloat16

EPS = 1e-6
N_META = 16
ROPE_THETA = 10000.0
MLA_HEADS = 8
QK_NOPE = 64
QK_ROPE = 32
V_HEAD = 64
QK_DIM = QK_NOPE + QK_ROPE
Q_LORA = 256
KV_LORA = 128
M_HEADS = 4
M_HEAD_DIM = 128
M_WIDTH = M_HEADS * M_HEAD_DIM
LANES = 128
MCHUNK = 128
META_PAD = MCHUNK - N_META
HALO = 16
NEG = -1e30
VMEM_LIMIT = 56 * 1024 * 1024


def _dot(a, b):
    return jnp.dot(a, b, preferred_element_type=F32)


def _dot_nt(a, b):
    return lax.dot_general(a, b, (((1,), (1,)), ((), ())), preferred_element_type=F32)


def _dot_tn(a, b):
    return lax.dot_general(a, b, (((0,), (0,)), ((), ())), preferred_element_type=F32)


def _sigmoid(x):
    return 1.0 / (1.0 + jnp.exp(-x))


def _rms(x, g):
    return x * lax.rsqrt(jnp.mean(x * x, axis=-1, keepdims=True) + EPS) * g


def _const_spec(shape):
    return pl.BlockSpec(shape, lambda *_: (0,) * len(shape))


def _in_proj_kernel(x_ref, cos_ref, sin_ref, n1_ref, w1_ref, b1_ref, wm_ref, bm_ref, wg_ref, bg_ref,
                    qn_ref, kvn_ref, wuq_ref, wuqr_ref, wuk_ref, wuv_ref, vone_ref,
                    q_out, k_out, v_out, mqk_out, mv_out, mo_out, gt_out, sga_out, sgb_out):
    xn = _rms(x_ref[...], n1_ref[...]).astype(BF16)
    cos = cos_ref[...]
    sin = sin_ref[...]

    p1 = _dot(xn, w1_ref[...]) + b1_ref[...]
    c_q = p1[:, :Q_LORA]
    c_kv = p1[:, Q_LORA:Q_LORA + KV_LORA]
    kr_plain = p1[:, Q_LORA + KV_LORA:Q_LORA + KV_LORA + LANES]
    kr_rot = p1[:, Q_LORA + KV_LORA + LANES:]
    k_rope = kr_plain * cos + kr_rot * sin

    cqn = (_rms(c_q, qn_ref[...]) * (QK_DIM ** -0.5)).astype(BF16)
    qa = _dot(cqn, wuq_ref[...])
    qb = _dot(cqn, wuqr_ref[...])
    ckvn = _rms(c_kv, kvn_ref[...]).astype(BF16)
    ka = _dot(ckvn, wuk_ref[...])
    for h in range(MLA_HEADS):
        sl = slice(h * LANES, (h + 1) * LANES)
        q_out[:, sl] = (qa[:, sl] * cos + qb[:, sl] * sin).astype(BF16)
        k_out[:, sl] = (ka[:, sl] + k_rope).astype(BF16)
    v_out[...] = (_dot(ckvn, wuv_ref[...]) + vone_ref[...]).astype(BF16)

    pm = _dot(xn, wm_ref[...]) + bm_ref[...]
    mqk_out[...] = pm[:, :2 * M_WIDTH].astype(BF16)
    mv_out[...] = pm[:, 2 * M_WIDTH:3 * M_WIDTH].astype(BF16)
    mo_out[...] = pm[:, 3 * M_WIDTH:4 * M_WIDTH].astype(BF16)
    gt_out[...] = pm[:, 4 * M_WIDTH:]

    pg = _dot(xn, wg_ref[...]) + bg_ref[...]
    d = sga_out.shape[-1]
    sga_out[...] = _sigmoid(pg[:, :d]).astype(BF16)
    sgb_out[...] = _sigmoid(pg[:, d:]).astype(BF16)


def _in_proj(x2, cos, sin, wts, tb):
    n, d = x2.shape
    widths = (MLA_HEADS * LANES, MLA_HEADS * LANES, MLA_HEADS * LANES, 2 * M_WIDTH, M_WIDTH, M_WIDTH, LANES, d, d)
    dtypes = (BF16, BF16, BF16, BF16, BF16, BF16, F32, BF16, BF16)
    row = lambda w: pl.BlockSpec((tb, w), lambda i: (i, 0))
    return pl.pallas_call(
        _in_proj_kernel,
        grid=(n // tb,),
        in_specs=[row(d), row(LANES), row(LANES)] + [_const_spec(w.shape) for w in wts],
        out_specs=[row(w) for w in widths],
        out_shape=[jax.ShapeDtypeStruct((n, w), t) for w, t in zip(widths, dtypes)],
        compiler_params=pltpu.CompilerParams(dimension_semantics=("parallel",), vmem_limit_bytes=VMEM_LIMIT),
        name="in_proj",
    )(x2, cos, sin, *wts)


def _attn_kernel(q_ref, k_ref, v_ref, km_ref, vm_ref, bias_ref, o_ref, m_scr, acc_scr, *, tk, nk):
    hsl = [slice(0, LANES), slice(LANES, 2 * LANES)]
    qs = [q_ref[:, sl] for sl in hsl]

    for hh, sl in enumerate(hsl):
        s = _dot_nt(qs[hh], km_ref[:, sl]) + bias_ref[...]
        m = jnp.max(s, axis=-1, keepdims=True)
        p = jnp.exp(s - m).astype(BF16)
        m_scr[hh] = m
        acc_scr[hh] = _dot(p, vm_ref[:, sl])

    def body(j, carry):
        off = pl.multiple_of(j * tk, tk)
        for hh, sl in enumerate(hsl):
            kc = k_ref[pl.ds(off, tk), sl]
            vc = v_ref[pl.ds(off, tk), sl]
            s = _dot_nt(qs[hh], kc)
            m_prev = m_scr[hh]
            m_new = jnp.maximum(m_prev, jnp.max(s, axis=-1, keepdims=True))
            alpha = jnp.exp(m_prev - m_new)
            p = jnp.exp(s - m_new).astype(BF16)
            acc_scr[hh] = alpha * acc_scr[hh] + _dot(p, vc)
            m_scr[hh] = m_new
        return carry

    lax.fori_loop(0, nk, body, 0)

    a0 = acc_scr[0]
    a1 = acc_scr[1]
    lane = lax.broadcasted_iota(jnp.int32, a0.shape, 1)
    o = jnp.where(lane < V_HEAD, a0 / a0[:, V_HEAD:V_HEAD + 1], a1 / a1[:, 0:1])
    o_ref[...] = o.astype(BF16)


def _attention(q, k, v, km, vm, bias, tq, tk):
    b, s, _ = q.shape
    hp = MLA_HEADS // 2
    w2 = 2 * LANES
    return pl.pallas_call(
        functools.partial(_attn_kernel, tk=tk, nk=s // tk),
        grid=(b, hp, s // tq),
        in_specs=[
            pl.BlockSpec((None, tq, w2), lambda bi, h, i: (bi, i, h)),
            pl.BlockSpec((None, s, w2), lambda bi, h, i: (bi, 0, h)),
            pl.BlockSpec((None, s, w2), lambda bi, h, i: (bi, 0, h)),
            pl.BlockSpec((MCHUNK, w2), lambda bi, h, i: (0, h)),
            pl.BlockSpec((MCHUNK, w2), lambda bi, h, i: (0, h)),
            pl.BlockSpec((1, MCHUNK), lambda bi, h, i: (0, 0)),
        ],
        out_specs=pl.BlockSpec((None, tq, LANES), lambda bi, h, i: (bi, i, h)),
        out_shape=jax.ShapeDtypeStruct((b, s, hp * LANES), BF16),
        scratch_shapes=[pltpu.VMEM((2, tq, 1), F32), pltpu.VMEM((2, tq, LANES), F32)],
        compiler_params=pltpu.CompilerParams(
            dimension_semantics=("parallel", "parallel", "arbitrary"), vmem_limit_bytes=VMEM_LIMIT),
        name="mla_attention",
    )(q, k, v, km, vm, bias)


def _conv_silu(cur, prev_row, next_row, cw, cb):
    n = cur.shape[0]
    ridx = lax.broadcasted_iota(jnp.int32, cur.shape, 0)
    x_prev = jnp.where(ridx == 0, prev_row, pltpu.roll(cur, 1, axis=0))
    x_next = jnp.where(ridx == n - 1, next_row, pltpu.roll(cur, n - 1, axis=0))
    y = cw[0:1, :] * x_prev + cw[1:2, :] * cur + cw[2:3, :] * x_next + cb
    return y * _sigmoid(y)


def _log_sigmoid(x):
    return jnp.minimum(x, 0.0) - jnp.log(1.0 + jnp.exp(-jnp.abs(x)))


def _tri(n, upper):
    r = lax.broadcasted_iota(jnp.int32, (n, n), 0)
    c = lax.broadcasted_iota(jnp.int32, (n, n), 1)
    return (c >= r) if upper else (c <= r)


def _v_aug(v_h):
    lane = lax.broadcasted_iota(jnp.int32, v_h.shape, 1)
    ones_col = jnp.where(lane == 0, 1.0, 0.0).astype(v_h.dtype)
    return jnp.concatenate([v_h, ones_col], axis=1)


def _state_update(k_h, v_aug, f_col, f_tot, b_col, c_ref, m_ref, h):
    g_col = f_tot - f_col + b_col
    g_max = jnp.max(g_col, axis=0, keepdims=True)
    kw = (k_h * jnp.exp(g_col - g_max)).astype(BF16)
    u = _dot_tn(kw, v_aug)
    m_prev = m_ref[h][0:1, 0:1]
    m_new = jnp.maximum(f_tot + m_prev, g_max)
    c_ref[h] = jnp.exp(f_tot + m_prev - m_new) * c_ref[h] + jnp.exp(g_max - m_new) * u
    m_ref[h] = jnp.broadcast_to(m_new, m_ref.shape[1:])


def _mlstm_chunk(q, k, v, gcol, grow, c_ref, m_ref, reverse):
    n = q.shape[0]
    gi = 2 * M_HEADS if reverse else 0
    gf = gi + M_HEADS
    tri_c = _tri(n, reverse).astype(F32)
    tri_r = _tri(n, not reverse).astype(F32)
    mask = _tri(n, reverse)
    f_cols = jnp.dot(tri_c, _log_sigmoid(gcol), precision=lax.Precision.HIGHEST, preferred_element_type=F32)
    f_rows = jnp.dot(_log_sigmoid(grow), tri_r, precision=lax.Precision.HIGHEST, preferred_element_type=F32)
    last = 0 if reverse else n - 1
    outs = []
    for h in range(M_HEADS):
        sl = slice(h * M_HEAD_DIM, (h + 1) * M_HEAD_DIM)
        f_col = f_cols[:, gf + h:gf + h + 1]
        f_row = f_rows[gf + h:gf + h + 1, :]
        b_col = gcol[:, gi + h:gi + h + 1]
        b_row = grow[gi + h:gi + h + 1, :]
        f_tot = f_col[last:last + 1, :]
        q_h = q[:, sl].astype(BF16)
        k_h = k[:, sl]
        v_aug = _v_aug(v[:, sl])

        log_d = jnp.where(mask, f_col - f_row + b_row, NEG)
        m_loc = jnp.max(log_d, axis=1, keepdims=True)
        s = (_dot_nt(q_h, k_h.astype(BF16)) * jnp.exp(log_d - m_loc)).astype(BF16)
        intra = _dot(s, v_aug)
        inter = _dot(q_h, c_ref[h].astype(BF16))
        m_prev = m_ref[h][0:1, 0:1]
        inter_log = f_col + m_prev
        m_j = jnp.maximum(inter_log, m_loc)
        hh = jnp.exp(inter_log - m_j) * inter + jnp.exp(m_loc - m_j) * intra
        den = hh[:, M_HEAD_DIM:M_HEAD_DIM + 1]
        outs.append(hh[:, :M_HEAD_DIM] / jnp.maximum(jnp.abs(den), jnp.exp(-m_j)))

        _state_update(k_h, v_aug, f_col, f_tot, b_col, c_ref, m_ref, h)
    return jnp.concatenate(outs, axis=1)


def _mlstm_kernel(qk_f, pv_f, nx_f, v_f, gc_f, gr_f, qk_b, pv_b, nx_b, v_b, gc_b, gr_b,
                  qk_m, v_m, gc_m, gr_m, cw_ref, cb_ref, hf_out, hb_out, cf_scr, mf_scr, cb_scr, mb_scr):
    j = pl.program_id(1)
    nc = pl.num_programs(1)
    cw = cw_ref[...]
    cb = cb_ref[...]
    scale = M_HEAD_DIM ** -0.5
    meta_last = qk_m[MCHUNK - 1:MCHUNK, :].astype(F32)

    @pl.when(j == 0)
    def _():
        cf_scr[...] = jnp.zeros_like(cf_scr)
        mf_scr[...] = jnp.zeros_like(mf_scr)
        cb_scr[...] = jnp.zeros_like(cb_scr)
        mb_scr[...] = jnp.zeros_like(mb_scr)
        ridx = lax.broadcasted_iota(jnp.int32, (MCHUNK, 1), 0)
        valid = ridx >= META_PAD
        cur = jnp.where(valid, qk_m[...].astype(F32), 0.0)
        first_x = qk_f[0:1, :].astype(F32)
        k_all = _conv_silu(cur, jnp.zeros_like(first_x), first_x, cw, cb)[:, M_WIDTH:] * scale
        gcol = gc_m[...]
        a_col = jnp.where(valid, _log_sigmoid(gcol), 0.0)
        f_cols = jnp.dot(_tri(MCHUNK, False).astype(F32), a_col, precision=lax.Precision.HIGHEST,
                         preferred_element_type=F32)
        for h in range(M_HEADS):
            sl = slice(h * M_HEAD_DIM, (h + 1) * M_HEAD_DIM)
            f_col = f_cols[:, M_HEADS + h:M_HEADS + h + 1]
            b_col = jnp.where(valid, gcol[:, h:h + 1], NEG)
            _state_update(k_all[:, sl], _v_aug(v_m[:, sl]), f_col, f_col[MCHUNK - 1:MCHUNK, :], b_col,
                          cf_scr, mf_scr, h)

    def run(qk_ref, pv_ref, nx_ref, v_ref, gc_ref, gr_ref, c, c_scr, m_scr, reverse, out_ref):
        prev_row = jnp.where(c == 0, meta_last, pv_ref[HALO - 1:HALO, :].astype(F32))
        next_row = jnp.where(c == nc - 1, 0.0, nx_ref[0:1, :].astype(F32))
        qk = _conv_silu(qk_ref[...].astype(F32), prev_row, next_row, cw, cb)
        h = _mlstm_chunk(qk[:, :M_WIDTH], qk[:, M_WIDTH:] * scale, v_ref[...], gc_ref[...], gr_ref[...],
                         c_scr, m_scr, reverse)
        out_ref[...] = h.astype(out_ref.dtype)

    run(qk_f, pv_f, nx_f, v_f, gc_f, gr_f, j, cf_scr, mf_scr, False, hf_out)
    run(qk_b, pv_b, nx_b, v_b, gc_b, gr_b, nc - 1 - j, cb_scr, mb_scr, True, hb_out)


def _mlstm(mqk, mv, gcol, grow, mqk_m, mv_m, gcol_m, grow_m, conv_w, conv_b):
    b, s, _ = mqk.shape
    nc = s // MCHUNK
    hpc = MCHUNK // HALO
    nhb = s // HALO

    def specs(chunk_of):
        return [
            pl.BlockSpec((None, MCHUNK, 2 * M_WIDTH), lambda bi, j: (bi, chunk_of(j), 0)),
            pl.BlockSpec((None, HALO, 2 * M_WIDTH), lambda bi, j: (bi, jnp.maximum(chunk_of(j) * hpc - 1, 0), 0)),
            pl.BlockSpec((None, HALO, 2 * M_WIDTH),
                         lambda bi, j: (bi, jnp.minimum((chunk_of(j) + 1) * hpc, nhb - 1), 0)),
            pl.BlockSpec((None, MCHUNK, M_WIDTH), lambda bi, j: (bi, chunk_of(j), 0)),
            pl.BlockSpec((None, MCHUNK, LANES), lambda bi, j: (bi, chunk_of(j), 0)),
            pl.BlockSpec((None, 4 * M_HEADS, MCHUNK), lambda bi, j: (bi, 0, chunk_of(j))),
        ]

    fwd = lambda j: j
    bwd = lambda j: nc - 1 - j
    operands = [mqk, mqk, mqk, mv, gcol, grow]
    out_spec = lambda chunk_of: pl.BlockSpec((None, MCHUNK, M_WIDTH), lambda bi, j: (bi, chunk_of(j), 0))
    return pl.pallas_call(
        _mlstm_kernel,
        grid=(b, nc),
        in_specs=specs(fwd) + specs(bwd) + [_const_spec(a.shape) for a in (mqk_m, mv_m, gcol_m, grow_m, conv_w, conv_b)],
        out_specs=[out_spec(fwd), out_spec(bwd)],
        out_shape=[jax.ShapeDtypeStruct((b, s, M_WIDTH), BF16)] * 2,
        scratch_shapes=[
            pltpu.VMEM((M_HEADS, M_HEAD_DIM, 2 * M_HEAD_DIM), F32), pltpu.VMEM((M_HEADS, 8, LANES), F32),
            pltpu.VMEM((M_HEADS, M_HEAD_DIM, 2 * M_HEAD_DIM), F32), pltpu.VMEM((M_HEADS, 8, LANES), F32),
        ],
        compiler_params=pltpu.CompilerParams(
            dimension_semantics=("parallel", "arbitrary"), vmem_limit_bytes=VMEM_LIMIT),
        name="mlstm",
    )(*operands, *operands, mqk_m, mv_m, gcol_m, grow_m, conv_w, conv_b)


def _merge_kernel(x_ref, a_ref, hf_ref, hb_ref, mo_ref, sga_ref, sgb_ref, mg_ref, wpa_ref, wpb_ref, wo_ref, o_ref):
    h = (hf_ref[...].astype(F32) + hb_ref[...].astype(F32)) * _sigmoid(mo_ref[...].astype(F32))
    mg = mg_ref[...]
    parts = []
    for hd in range(M_HEADS):
        sl = slice(hd * M_HEAD_DIM, (hd + 1) * M_HEAD_DIM)
        parts.append(_rms(h[:, sl], mg[:, sl]).astype(BF16))
    m_out = jnp.concatenate(parts, axis=1)
    merged = (sga_ref[...].astype(F32) * _dot(a_ref[...], wpa_ref[...])
              + sgb_ref[...].astype(F32) * _dot(m_out, wpb_ref[...]))
    o_ref[...] = x_ref[...] + _dot(merged.astype(BF16), wo_ref[...])


def _merge(x2, a2, hf2, hb2, mo2, sga2, sgb2, wts, tb):
    n, d = x2.shape
    row = lambda w: pl.BlockSpec((tb, w), lambda i: (i, 0))
    return pl.pallas_call(
        _merge_kernel,
        grid=(n // tb,),
        in_specs=[row(d), row(a2.shape[1]), row(M_WIDTH), row(M_WIDTH), row(M_WIDTH), row(d), row(d)]
        + [_const_spec(w.shape) for w in wts],
        out_specs=row(d),
        out_shape=jax.ShapeDtypeStruct((n, d), F32),
        compiler_params=pltpu.CompilerParams(dimension_semantics=("parallel",), vmem_limit_bytes=VMEM_LIMIT),
        name="merge",
    )(x2, a2, hf2, hb2, mo2, sga2, sgb2, *wts)


def _ffn_kernel(x_ref, n2_ref, wg_ref, wu_ref, wd_ref, fn_ref, o_ref):
    x = x_ref[...]
    xn = _rms(x, n2_ref[...]).astype(BF16)
    g = _dot(xn, wg_ref[...])
    u = _dot(xn, wu_ref[...])
    act = (g * _sigmoid(g) * u).astype(BF16)
    y = x + _dot(act, wd_ref[...])
    o_ref[...] = _rms(y, fn_ref[...])


def _ffn(x2, wts, tb):
    n, d = x2.shape
    row = pl.BlockSpec((tb, d), lambda i: (i, 0))
    return pl.pallas_call(
        _ffn_kernel,
        grid=(n // tb,),
        in_specs=[row] + [_const_spec(w.shape) for w in wts],
        out_specs=row,
        out_shape=jax.ShapeDtypeStruct((n, d), F32),
        compiler_params=pltpu.CompilerParams(dimension_semantics=("parallel",), vmem_limit_bytes=VMEM_LIMIT),
        name="ffn",
    )(x2, *wts)


def _rope_tables(pos):
    half = QK_ROPE // 2
    freqs = ROPE_THETA ** (-jnp.arange(half, dtype=F32) / half)
    ang = pos.astype(F32)[:, None] * freqs[None, :]
    c, s = jnp.cos(ang), jnp.sin(ang)
    n = pos.shape[0]
    pad = jnp.zeros((n, LANES - QK_DIM), F32)
    cos = jnp.concatenate([jnp.ones((n, QK_NOPE), F32), c, c, pad], axis=1)
    sin = jnp.concatenate([jnp.zeros((n, QK_NOPE), F32), s, s, pad], axis=1)
    return cos, sin


def _rot(w):
    half = QK_ROPE // 2
    return jnp.concatenate([-w[..., half:], w[..., :half]], axis=-1)


def _pad_cols(w, left, total):
    return jnp.pad(w, [(0, 0)] * (w.ndim - 1) + [(left, total - left - w.shape[-1])])


def _prep_weights(norm1_g, w_in, b_in, q_norm_g, kv_norm_g, w_uq, w_ukv):
    d = w_in.shape[0]
    o_kr = Q_LORA + KV_LORA
    o_m = o_kr + QK_ROPE
    o_g = o_m + 4 * M_WIDTH
    o_ga = o_g + 4 * M_HEADS

    def seg1(w):
        kr = w[..., o_kr:o_m]
        return jnp.concatenate([w[..., :o_kr], _pad_cols(kr, QK_NOPE, LANES), _pad_cols(_rot(kr), QK_NOPE, LANES)], axis=-1)

    def segm(w):
        return jnp.concatenate([w[..., o_m:o_g], _pad_cols(w[..., o_g:o_ga], 0, LANES)], axis=-1)

    b2 = b_in[None, :]
    w1, b1 = seg1(w_in).astype(BF16), seg1(b2)
    wm, bm = segm(w_in).astype(BF16), segm(b2)
    wg, bg = w_in[:, o_ga:].astype(BF16), b2[:, o_ga:]

    uq = w_uq.reshape(Q_LORA, MLA_HEADS, QK_DIM)
    wuq = _pad_cols(uq, 0, LANES).reshape(Q_LORA, MLA_HEADS * LANES).astype(BF16)
    wuqr = _pad_cols(_rot(uq[..., QK_NOPE:]), QK_NOPE, LANES).reshape(Q_LORA, MLA_HEADS * LANES).astype(BF16)
    ukv = w_ukv.reshape(KV_LORA, MLA_HEADS, QK_NOPE + V_HEAD)
    wuk = _pad_cols(ukv[..., :QK_NOPE], 0, LANES).reshape(KV_LORA, MLA_HEADS * LANES).astype(BF16)
    uv = ukv[..., QK_NOPE:].reshape(KV_LORA, MLA_HEADS // 2, 2, V_HEAD)
    wuv = jnp.stack([_pad_cols(uv[:, :, 0], 0, LANES), _pad_cols(uv[:, :, 1], V_HEAD, LANES)], axis=2)
    wuv = wuv.reshape(KV_LORA, MLA_HEADS * LANES).astype(BF16)
    lane = jnp.arange(2 * LANES)
    vone = jnp.tile(((lane == V_HEAD) | (lane == LANES)).astype(F32), MLA_HEADS // 2)[None, :]
    return (norm1_g[None, :], w1, b1, wm, bm, wg, bg, q_norm_g[None, :], kv_norm_g[None, :], wuq, wuqr, wuk, wuv, vone)


def _pick(n, pref):
    return pref if n % pref == 0 else n


def _trunk(x, meta_parts, in_wts, conv_w, conv_b, merge_wts, ffn_wts):
    b, s, d = x.shape
    n = b * s
    tb = _pick(n, 512)
    k_m, v_m, mqk_m, mv_m, gt_m = meta_parts

    cos, sin = _rope_tables(N_META + jnp.arange(s))
    if b > 1:
        cos, sin = jnp.tile(cos, (b, 1)), jnp.tile(sin, (b, 1))
    x2 = x.reshape(n, d)
    q, k, v, mqk, mv, mo, gt, sga, sgb = _in_proj(x2, cos, sin, in_wts, tb)

    r3 = lambda a: a.reshape(b, s, a.shape[-1])
    bias = jnp.where(jnp.arange(MCHUNK) >= META_PAD, 0.0, NEG).astype(F32)[None, :]
    a_out = _attention(r3(q), r3(k), r3(v), k_m, v_m, bias, _pick(s, 512), _pick(s, 512))

    gt3 = r3(gt)
    grow = jnp.swapaxes(gt3[..., :4 * M_HEADS], 1, 2)
    grow_m = jnp.swapaxes(gt_m[:, :4 * M_HEADS], 0, 1)
    h_f, h_b = _mlstm(r3(mqk), r3(mv), gt3, grow, mqk_m, mv_m, gt_m, grow_m, conv_w, conv_b)

    h1 = _merge(x2, a_out.reshape(n, -1), h_f.reshape(n, -1), h_b.reshape(n, -1), mo, sga, sgb, merge_wts, tb)
    return _ffn(h1, ffn_wts, tb).reshape(b, s, d)


def kernel(x_prompt, x_sample, meta_tokens, norm1_g, w_in, b_in, conv_w, conv_b, q_norm_g, kv_norm_g, w_uq, w_ukv, m_norm_g, w_pa, w_pb, w_o, norm2_g, w_ffn_gate, w_ffn_up, w_ffn_down, final_norm_g):
    assert w_in.shape[0] == 1, "single-layer trunk"
    d = x_prompt.shape[-1]
    in_wts = _prep_weights(norm1_g[0], w_in[0], b_in[0], q_norm_g[0], kv_norm_g[0], w_uq[0], w_ukv[0])
    merge_wts = (m_norm_g[0][None, :], w_pa[0].astype(BF16), w_pb[0].astype(BF16), w_o[0].astype(BF16))
    ffn_wts = (norm2_g[0][None, :], w_ffn_gate[0].astype(BF16), w_ffn_up[0].astype(BF16),
               w_ffn_down[0].astype(BF16), final_norm_g[None, :])
    cw, cb = conv_w[0], conv_b[0][None, :]

    hm = jnp.concatenate([jnp.zeros((META_PAD, d), x_prompt.dtype), meta_tokens.astype(x_prompt.dtype)], axis=0)
    cos_m, sin_m = _rope_tables(jnp.arange(MCHUNK) - META_PAD)
    _, k_m, v_m, mqk_m, mv_m, _, gt_m, _, _ = _in_proj(hm, cos_m, sin_m, in_wts, MCHUNK)
    meta_parts = (k_m, v_m, mqk_m, mv_m, gt_m)

    y_prompt = _trunk(x_prompt, meta_parts, in_wts, cw, cb, merge_wts, ffn_wts)
    y_sample = _trunk(x_sample, meta_parts, in_wts, cw, cb, merge_wts, ffn_wts)
    return (y_prompt, y_sample)
```

```python
import functools

import jax
import jax.numpy as jnp
from jax import lax
from jax.experimental import pallas as pl
from jax.experimental.pallas import tpu as pltpu

F32 = jnp.float32
BF16 = jnp.bfloat16

EPS = 1e-6
N_META = 16
ROPE_THETA = 10000.0
MLA_HEADS = 8
QK_NOPE = 64
QK_ROPE = 32
V_HEAD = 64
QK_DIM = QK_NOPE + QK_ROPE
Q_LORA = 256
KV_LORA = 128
M_HEADS = 4
M_HEAD_DIM = 128
M_WIDTH = M_HEADS * M_HEAD_DIM
LANES = 128
MCHUNK = 128
META_PAD = MCHUNK - N_META
HALO = 16
NEG = -1e30
LOG2E = 1.4426950408889634
VMEM_LIMIT = 56 * 1024 * 1024


def _dot(a, b):
    return jnp.dot(a, b, preferred_element_type=F32)


def _dot_nt(a, b):
    return lax.dot_general(a, b, (((1,), (1,)), ((), ())), preferred_element_type=F32)


def _dot_tn(a, b):
    return lax.dot_general(a, b, (((0,), (0,)), ((), ())), preferred_element_type=F32)


def _sigmoid(x):
    return 1.0 / (1.0 + jnp.exp(-x))


def _rms(x, g):
    return x * lax.rsqrt(jnp.mean(x * x, axis=-1, keepdims=True) + EPS) * g


def _const_spec(shape):
    return pl.BlockSpec(shape, lambda *_: (0,) * len(shape))


def _in_proj_kernel(x_ref, cos_ref, sin_ref, n1_ref, w1_ref, b1_ref, wm_ref, bm_ref, wg_ref, bg_ref,
                    qn_ref, kvn_ref, wuq_ref, wuqr_ref, wuk_ref, wuv_ref, vone_ref,
                    q_out, k_out, v_out, mqk_out, mv_out, mo_out, gt_out, sga_out, sgb_out):
    xn = _rms(x_ref[...], n1_ref[...]).astype(BF16)
    cos = cos_ref[...]
    sin = sin_ref[...]

    p1 = _dot(xn, w1_ref[...]) + b1_ref[...]
    c_q = p1[:, :Q_LORA]
    c_kv = p1[:, Q_LORA:Q_LORA + KV_LORA]
    kr_plain = p1[:, Q_LORA + KV_LORA:Q_LORA + KV_LORA + LANES]
    kr_rot = p1[:, Q_LORA + KV_LORA + LANES:]
    k_rope = kr_plain * cos + kr_rot * sin

    cqn = (_rms(c_q, qn_ref[...]) * (QK_DIM ** -0.5 * LOG2E)).astype(BF16)
    qa = _dot(cqn, wuq_ref[...])
    qb = _dot(cqn, wuqr_ref[...])
    ckvn = _rms(c_kv, kvn_ref[...]).astype(BF16)
    ka = _dot(ckvn, wuk_ref[...])
    for h in range(MLA_HEADS):
        sl = slice(h * LANES, (h + 1) * LANES)
        q_out[:, sl] = (qa[:, sl] * cos + qb[:, sl] * sin).astype(BF16)
        k_out[:, sl] = (ka[:, sl] + k_rope).astype(BF16)
    v_out[...] = (_dot_nt(wuv_ref[...], ckvn) + vone_ref[...]).astype(BF16)

    pm = _dot(xn, wm_ref[...]) + bm_ref[...]
    mqk_out[...] = pm[:, :2 * M_WIDTH].astype(BF16)
    mv_out[...] = pm[:, 2 * M_WIDTH:3 * M_WIDTH].astype(BF16)
    mo_out[...] = pm[:, 3 * M_WIDTH:4 * M_WIDTH].astype(BF16)
    gt_out[...] = pm[:, 4 * M_WIDTH:]

    pg = _dot(xn, wg_ref[...]) + bg_ref[...]
    d = sga_out.shape[-1]
    sga_out[...] = _sigmoid(pg[:, :d]).astype(BF16)
    sgb_out[...] = _sigmoid(pg[:, d:]).astype(BF16)


def _in_proj(x2, cos, sin, wts, tb):
    n, d = x2.shape
    widths = (MLA_HEADS * LANES, MLA_HEADS * LANES, MLA_HEADS * LANES, 2 * M_WIDTH, M_WIDTH, M_WIDTH, LANES, d, d)
    dtypes = (BF16, BF16, BF16, BF16, BF16, BF16, F32, BF16, BF16)
    row = lambda w: pl.BlockSpec((tb, w), lambda i: (i, 0))
    out_specs = [row(w) for w in widths]
    out_shape = [jax.ShapeDtypeStruct((n, w), t) for w, t in zip(widths, dtypes)]
    out_specs[2] = pl.BlockSpec((None, widths[2], tb), lambda i: (i, 0, 0))
    out_shape[2] = jax.ShapeDtypeStruct((n // tb, widths[2], tb), BF16)
    return pl.pallas_call(
        _in_proj_kernel,
        grid=(n // tb,),
        in_specs=[row(d), row(LANES), row(LANES)] + [_const_spec(w.shape) for w in wts],
        out_specs=out_specs,
        out_shape=out_shape,
        compiler_params=pltpu.CompilerParams(dimension_semantics=("parallel",), vmem_limit_bytes=VMEM_LIMIT),
        name="in_proj",
    )(x2, cos, sin, *wts)


def _attn_kernel(q_ref, k_ref, vt_ref, km_ref, vtm_ref, bias_ref, o_ref,
                 m_scr, acc_scr, mx_scr, al_scr, s_scr, p_scr, *, nk):
    hsl = [slice(0, LANES), slice(LANES, 2 * LANES)]
    qs = [q_ref[:, sl] for sl in hsl]

    for hh, sl in enumerate(hsl):
        s = _dot_nt(km_ref[:, sl], qs[hh]) + bias_ref[...]
        m = jnp.max(s, axis=0, keepdims=True)
        p = jnp.exp2(s - m).astype(BF16)
        m_scr[hh] = m
        acc_scr[hh] = _dot(vtm_ref[sl, :], p)

    def scores(c, slot):
        for hh, sl in enumerate(hsl):
            s = _dot_nt(k_ref[c, :, sl], qs[hh])
            mx_scr[slot, hh] = jnp.max(s, axis=0, keepdims=True)
            s_scr[slot, hh] = s

    def softmax(slot):
        for hh in range(2):
            m_prev = m_scr[hh]
            m_new = jnp.maximum(m_prev, mx_scr[slot, hh])
            al_scr[slot, hh] = jnp.exp2(m_prev - m_new)
            m_scr[hh] = m_new
            p_scr[slot, hh] = jnp.exp2(s_scr[slot, hh] - m_new).astype(BF16)

    def values(c, slot):
        for hh, sl in enumerate(hsl):
            acc_scr[hh] = al_scr[slot, hh] * acc_scr[hh] + _dot(vt_ref[c, sl, :], p_scr[slot, hh])

    scores(0, 0)
    scores(1, 1)
    softmax(0)

    def body(t, carry):
        i = 2 * t
        scores(i + 2, 0)
        softmax(1)
        values(i, 0)
        scores(i + 3, 1)
        softmax(0)
        values(i + 1, 1)
        return carry

    lax.fori_loop(0, nk // 2 - 1, body, 0)
    softmax(1)
    values(nk - 2, 0)
    values(nk - 1, 1)

    a0 = acc_scr[0]
    a1 = acc_scr[1]
    row = lax.broadcasted_iota(jnp.int32, a0.shape, 0)
    o_t = jnp.where(row < V_HEAD, a0 / a0[V_HEAD:V_HEAD + 1, :], a1 / a1[0:1, :])
    o_ref[...] = o_t.T.astype(BF16)


def _attention(q, k, vt, km, vtm, bias, tq):
    b, s, _ = q.shape
    nk = k.shape[0] // b
    assert nk >= 2 and nk % 2 == 0, "the key-chunk pipeline is unrolled by two"
    tk = k.shape[1]
    hp = MLA_HEADS // 2
    w2 = 2 * LANES
    return pl.pallas_call(
        functools.partial(_attn_kernel, nk=nk),
        grid=(b, hp, s // tq),
        in_specs=[
            pl.BlockSpec((None, tq, w2), lambda bi, h, i: (bi, i, h)),
            pl.BlockSpec((nk, tk, w2), lambda bi, h, i: (bi, 0, h)),
            pl.BlockSpec((nk, w2, tk), lambda bi, h, i: (bi, h, 0)),
            pl.BlockSpec((MCHUNK, w2), lambda bi, h, i: (0, h)),
            pl.BlockSpec((w2, MCHUNK), lambda bi, h, i: (h, 0)),
            pl.BlockSpec((MCHUNK, 1), lambda bi, h, i: (0, 0)),
        ],
        out_specs=pl.BlockSpec((None, tq, LANES), lambda bi, h, i: (bi, i, h)),
        out_shape=jax.ShapeDtypeStruct((b, s, hp * LANES), BF16),
        scratch_shapes=[
            pltpu.VMEM((2, 1, tq), F32), pltpu.VMEM((2, LANES, tq), F32),
            pltpu.VMEM((2, 2, 1, tq), F32), pltpu.VMEM((2, 2, 1, tq), F32),
            pltpu.VMEM((2, 2, tk, tq), F32), pltpu.VMEM((2, 2, tk, tq), BF16),
        ],
        compiler_params=pltpu.CompilerParams(
            dimension_semantics=("parallel", "parallel", "arbitrary"), vmem_limit_bytes=VMEM_LIMIT),
        name="mla_attention",
    )(q, k, vt, km, vtm, bias)


def _conv_silu(cur, prev_row, next_row, cw, cb):
    n = cur.shape[0]
    ridx = lax.broadcasted_iota(jnp.int32, cur.shape, 0)
    x_prev = jnp.where(ridx == 0, prev_row, pltpu.roll(cur, 1, axis=0))
    x_next = jnp.where(ridx == n - 1, next_row, pltpu.roll(cur, n - 1, axis=0))
    y = cw[0:1, :] * x_prev + cw[1:2, :] * cur + cw[2:3, :] * x_next + cb
    return y * _sigmoid(y)


def _log_sigmoid(x):
    return jnp.minimum(x, 0.0) - jnp.log(1.0 + jnp.exp(-jnp.abs(x)))


def _tri(n, upper):
    r = lax.broadcasted_iota(jnp.int32, (n, n), 0)
    c = lax.broadcasted_iota(jnp.int32, (n, n), 1)
    return (c >= r) if upper else (c <= r)


def _v_aug(v_h):
    lane = lax.broadcasted_iota(jnp.int32, v_h.shape, 1)
    ones_col = jnp.where(lane == 0, 1.0, 0.0).astype(v_h.dtype)
    return jnp.concatenate([v_h, ones_col], axis=1)


def _state_update(k_h, v_aug, f_col, f_tot, b_col, c_ref, m_ref, h):
    g_col = f_tot - f_col + b_col
    g_max = jnp.max(g_col, axis=0, keepdims=True)
    kw = (k_h * jnp.exp(g_col - g_max)).astype(BF16)
    u = _dot_tn(kw, v_aug)
    m_prev = m_ref[h][0:1, 0:1]
    m_new = jnp.maximum(f_tot + m_prev, g_max)
    c_ref[h] = jnp.exp(f_tot + m_prev - m_new) * c_ref[h] + jnp.exp(g_max - m_new) * u
    m_ref[h] = jnp.broadcast_to(m_new, m_ref.shape[1:])


def _mlstm_chunk(q, k, v, gcol, grow, c_ref, m_ref, reverse):
    n = q.shape[0]
    gi = 2 * M_HEADS if reverse else 0
    gf = gi + M_HEADS
    tri_c = _tri(n, reverse).astype(F32)
    tri_r = _tri(n, not reverse).astype(F32)
    mask = _tri(n, reverse)
    f_cols = jnp.dot(tri_c, _log_sigmoid(gcol), precision=lax.Precision.HIGHEST, preferred_element_type=F32)
    f_rows = jnp.dot(_log_sigmoid(grow), tri_r, precision=lax.Precision.HIGHEST, preferred_element_type=F32)
    last = 0 if reverse else n - 1
    outs = []
    for h in range(M_HEADS):
        sl = slice(h * M_HEAD_DIM, (h + 1) * M_HEAD_DIM)
        f_col = f_cols[:, gf + h:gf + h + 1]
        f_row = f_rows[gf + h:gf + h + 1, :]
        b_col = gcol[:, gi + h:gi + h + 1]
        b_row = grow[gi + h:gi + h + 1, :]
        f_tot = f_col[last:last + 1, :]
        q_h = q[:, sl].astype(BF16)
        k_h = k[:, sl]
        v_aug = _v_aug(v[:, sl])

        log_d = jnp.where(mask, f_col - f_row + b_row, NEG)
        m_loc = jnp.max(log_d, axis=1, keepdims=True)
        s = (_dot_nt(q_h, k_h.astype(BF16)) * jnp.exp(log_d - m_loc)).astype(BF16)
        intra = _dot(s, v_aug)
        inter = _dot(q_h, c_ref[h].astype(BF16))
        m_prev = m_ref[h][0:1, 0:1]
        inter_log = f_col + m_prev
        m_j = jnp.maximum(inter_log, m_loc)
        hh = jnp.exp(inter_log - m_j) * inter + jnp.exp(m_loc - m_j) * intra
        den = hh[:, M_HEAD_DIM:M_HEAD_DIM + 1]
        outs.append(hh[:, :M_HEAD_DIM] / jnp.maximum(jnp.abs(den), jnp.exp(-m_j)))

        _state_update(k_h, v_aug, f_col, f_tot, b_col, c_ref, m_ref, h)
    return jnp.concatenate(outs, axis=1)


def _mlstm_kernel(qk_f, pv_f, nx_f, v_f, gc_f, gr_f, qk_b, pv_b, nx_b, v_b, gc_b, gr_b,
                  qk_m, v_m, gc_m, gr_m, cw_ref, cb_ref, hf_out, hb_out, cf_scr, mf_scr, cb_scr, mb_scr):
    j = pl.program_id(1)
    nc = pl.num_programs(1)
    cw = cw_ref[...]
    cb = cb_ref[...]
    scale = M_HEAD_DIM ** -0.5
    meta_last = qk_m[MCHUNK - 1:MCHUNK, :].astype(F32)

    @pl.when(j == 0)
    def _():
        cf_scr[...] = jnp.zeros_like(cf_scr)
        mf_scr[...] = jnp.zeros_like(mf_scr)
        cb_scr[...] = jnp.zeros_like(cb_scr)
        mb_scr[...] = jnp.zeros_like(mb_scr)
        ridx = lax.broadcasted_iota(jnp.int32, (MCHUNK, 1), 0)
        valid = ridx >= META_PAD
        cur = jnp.where(valid, qk_m[...].astype(F32), 0.0)
        first_x = qk_f[0:1, :].astype(F32)
        k_all = _conv_silu(cur, jnp.zeros_like(first_x), first_x, cw, cb)[:, M_WIDTH:] * scale
        gcol = gc_m[...]
        a_col = jnp.where(valid, _log_sigmoid(gcol), 0.0)
        f_cols = jnp.dot(_tri(MCHUNK, False).astype(F32), a_col, precision=lax.Precision.HIGHEST,
                         preferred_element_type=F32)
        for h in range(M_HEADS):
            sl = slice(h * M_HEAD_DIM, (h + 1) * M_HEAD_DIM)
            f_col = f_cols[:, M_HEADS + h:M_HEADS + h + 1]
            b_col = jnp.where(valid, gcol[:, h:h + 1], NEG)
            _state_update(k_all[:, sl], _v_aug(v_m[:, sl]), f_col, f_col[MCHUNK - 1:MCHUNK, :], b_col,
                          cf_scr, mf_scr, h)

    def run(qk_ref, pv_ref, nx_ref, v_ref, gc_ref, gr_ref, c, c_scr, m_scr, reverse, out_ref):
        prev_row = jnp.where(c == 0, meta_last, pv_ref[HALO - 1:HALO, :].astype(F32))
        next_row = jnp.where(c == nc - 1, 0.0, nx_ref[0:1, :].astype(F32))
        qk = _conv_silu(qk_ref[...].astype(F32), prev_row, next_row, cw, cb)
        h = _mlstm_chunk(qk[:, :M_WIDTH], qk[:, M_WIDTH:] * scale, v_ref[...], gc_ref[...], gr_ref[...],
                         c_scr, m_scr, reverse)
        out_ref[...] = h.astype(out_ref.dtype)

    run(qk_f, pv_f, nx_f, v_f, gc_f, gr_f, j, cf_scr, mf_scr, False, hf_out)
    run(qk_b, pv_b, nx_b, v_b, gc_b, gr_b, nc - 1 - j, cb_scr, mb_scr, True, hb_out)


def _mlstm(mqk, mv, gcol, grow, mqk_m, mv_m, gcol_m, grow_m, conv_w, conv_b):
    b, s, _ = mqk.shape
    nc = s // MCHUNK
    hpc = MCHUNK // HALO
    nhb = s // HALO

    def specs(chunk_of):
        return [
            pl.BlockSpec((None, MCHUNK, 2 * M_WIDTH), lambda bi, j: (bi, chunk_of(j), 0)),
            pl.BlockSpec((None, HALO, 2 * M_WIDTH), lambda bi, j: (bi, jnp.maximum(chunk_of(j) * hpc - 1, 0), 0)),
            pl.BlockSpec((None, HALO, 2 * M_WIDTH),
                         lambda bi, j: (bi, jnp.minimum((chunk_of(j) + 1) * hpc, nhb - 1), 0)),
            pl.BlockSpec((None, MCHUNK, M_WIDTH), lambda bi, j: (bi, chunk_of(j), 0)),
            pl.BlockSpec((None, MCHUNK, LANES), lambda bi, j: (bi, chunk_of(j), 0)),
            pl.BlockSpec((None, 4 * M_HEADS, MCHUNK), lambda bi, j: (bi, 0, chunk_of(j))),
        ]

    fwd = lambda j: j
    bwd = lambda j: nc - 1 - j
    operands = [mqk, mqk, mqk, mv, gcol, grow]
    out_spec = lambda chunk_of: pl.BlockSpec((None, MCHUNK, M_WIDTH), lambda bi, j: (bi, chunk_of(j), 0))
    return pl.pallas_call(
        _mlstm_kernel,
        grid=(b, nc),
        in_specs=specs(fwd) + specs(bwd) + [_const_spec(a.shape) for a in (mqk_m, mv_m, gcol_m, grow_m, conv_w, conv_b)],
        out_specs=[out_spec(fwd), out_spec(bwd)],
        out_shape=[jax.ShapeDtypeStruct((b, s, M_WIDTH), BF16)] * 2,
        scratch_shapes=[
            pltpu.VMEM((M_HEADS, M_HEAD_DIM, 2 * M_HEAD_DIM), F32), pltpu.VMEM((M_HEADS, 8, LANES), F32),
            pltpu.VMEM((M_HEADS, M_HEAD_DIM, 2 * M_HEAD_DIM), F32), pltpu.VMEM((M_HEADS, 8, LANES), F32),
        ],
        compiler_params=pltpu.CompilerParams(
            dimension_semantics=("parallel", "arbitrary"), vmem_limit_bytes=VMEM_LIMIT),
        name="mlstm",
    )(*operands, *operands, mqk_m, mv_m, gcol_m, grow_m, conv_w, conv_b)


def _merge_kernel(x_ref, a_ref, hf_ref, hb_ref, mo_ref, sga_ref, sgb_ref, mg_ref, wpa_ref, wpb_ref, wo_ref, o_ref):
    h = (hf_ref[...].astype(F32) + hb_ref[...].astype(F32)) * _sigmoid(mo_ref[...].astype(F32))
    mg = mg_ref[...]
    parts = []
    for hd in range(M_HEADS):
        sl = slice(hd * M_HEAD_DIM, (hd + 1) * M_HEAD_DIM)
        parts.append(_rms(h[:, sl], mg[:, sl]).astype(BF16))
    m_out = jnp.concatenate(parts, axis=1)
    merged = (sga_ref[...].astype(F32) * _dot(a_ref[...], wpa_ref[...])
              + sgb_ref[...].astype(F32) * _dot(m_out, wpb_ref[...]))
    o_ref[...] = x_ref[...] + _dot(merged.astype(BF16), wo_ref[...])


def _merge(x2, a2, hf2, hb2, mo2, sga2, sgb2, wts, tb):
    n, d = x2.shape
    row = lambda w: pl.BlockSpec((tb, w), lambda i: (i, 0))
    return pl.pallas_call(
        _merge_kernel,
        grid=(n // tb,),
        in_specs=[row(d), row(a2.shape[1]), row(M_WIDTH), row(M_WIDTH), row(M_WIDTH), row(d), row(d)]
        + [_const_spec(w.shape) for w in wts],
        out_specs=row(d),
        out_shape=jax.ShapeDtypeStruct((n, d), F32),
        compiler_params=pltpu.CompilerParams(dimension_semantics=("parallel",), vmem_limit_bytes=VMEM_LIMIT),
        name="merge",
    )(x2, a2, hf2, hb2, mo2, sga2, sgb2, *wts)


def _ffn_kernel(x_ref, n2_ref, wg_ref, wu_ref, wd_ref, fn_ref, o_ref):
    x = x_ref[...]
    xn = _rms(x, n2_ref[...]).astype(BF16)
    g = _dot(xn, wg_ref[...])
    u = _dot(xn, wu_ref[...])
    act = (g * _sigmoid(g) * u).astype(BF16)
    y = x + _dot(act, wd_ref[...])
    o_ref[...] = _rms(y, fn_ref[...])


def _ffn(x2, wts, tb):
    n, d = x2.shape
    row = pl.BlockSpec((tb, d), lambda i: (i, 0))
    return pl.pallas_call(
        _ffn_kernel,
        grid=(n // tb,),
        in_specs=[row] + [_const_spec(w.shape) for w in wts],
        out_specs=row,
        out_shape=jax.ShapeDtypeStruct((n, d), F32),
        compiler_params=pltpu.CompilerParams(dimension_semantics=("parallel",), vmem_limit_bytes=VMEM_LIMIT),
        name="ffn",
    )(x2, *wts)


def _rope_tables(pos):
    half = QK_ROPE // 2
    freqs = ROPE_THETA ** (-jnp.arange(half, dtype=F32) / half)
    ang = pos.astype(F32)[:, None] * freqs[None, :]
    c, s = jnp.cos(ang), jnp.sin(ang)
    n = pos.shape[0]
    pad = jnp.zeros((n, LANES - QK_DIM), F32)
    cos = jnp.concatenate([jnp.ones((n, QK_NOPE), F32), c, c, pad], axis=1)
    sin = jnp.concatenate([jnp.zeros((n, QK_NOPE), F32), s, s, pad], axis=1)
    return cos, sin


def _rot(w):
    half = QK_ROPE // 2
    return jnp.concatenate([-w[..., half:], w[..., :half]], axis=-1)


def _pad_cols(w, left, total):
    return jnp.pad(w, [(0, 0)] * (w.ndim - 1) + [(left, total - left - w.shape[-1])])


def _prep_weights(norm1_g, w_in, b_in, q_norm_g, kv_norm_g, w_uq, w_ukv):
    d = w_in.shape[0]
    o_kr = Q_LORA + KV_LORA
    o_m = o_kr + QK_ROPE
    o_g = o_m + 4 * M_WIDTH
    o_ga = o_g + 4 * M_HEADS

    def seg1(w):
        kr = w[..., o_kr:o_m]
        return jnp.concatenate([w[..., :o_kr], _pad_cols(kr, QK_NOPE, LANES), _pad_cols(_rot(kr), QK_NOPE, LANES)], axis=-1)

    def segm(w):
        return jnp.concatenate([w[..., o_m:o_g], _pad_cols(w[..., o_g:o_ga], 0, LANES)], axis=-1)

    b2 = b_in[None, :]
    w1, b1 = seg1(w_in).astype(BF16), seg1(b2)
    wm, bm = segm(w_in).astype(BF16), segm(b2)
    wg, bg = w_in[:, o_ga:].astype(BF16), b2[:, o_ga:]

    uq = w_uq.reshape(Q_LORA, MLA_HEADS, QK_DIM)
    wuq = _pad_cols(uq, 0, LANES).reshape(Q_LORA, MLA_HEADS * LANES).astype(BF16)
    wuqr = _pad_cols(_rot(uq[..., QK_NOPE:]), QK_NOPE, LANES).reshape(Q_LORA, MLA_HEADS * LANES).astype(BF16)
    ukv = w_ukv.reshape(KV_LORA, MLA_HEADS, QK_NOPE + V_HEAD)
    wuk = _pad_cols(ukv[..., :QK_NOPE], 0, LANES).reshape(KV_LORA, MLA_HEADS * LANES).astype(BF16)
    uv = ukv[..., QK_NOPE:].reshape(KV_LORA, MLA_HEADS // 2, 2, V_HEAD)
    wuv = jnp.stack([_pad_cols(uv[:, :, 0], 0, LANES), _pad_cols(uv[:, :, 1], V_HEAD, LANES)], axis=2)
    wuv = wuv.reshape(KV_LORA, MLA_HEADS * LANES).T.astype(BF16)
    lane = jnp.arange(2 * LANES)
    vone = jnp.tile(((lane == V_HEAD) | (lane == LANES)).astype(F32), MLA_HEADS // 2)[:, None]
    return (norm1_g[None, :], w1, b1, wm, bm, wg, bg, q_norm_g[None, :], kv_norm_g[None, :], wuq, wuqr, wuk, wuv, vone)


def _pick(n, pref):
    return pref if n % pref == 0 else n


def _trunk(x, meta_parts, in_wts, conv_w, conv_b, merge_wts, ffn_wts):
    b, s, d = x.shape
    n = b * s
    tb = _pick(n, 512)
    k_m, vt_m, mqk_m, mv_m, gt_m = meta_parts

    cos, sin = _rope_tables(N_META + jnp.arange(s))
    if b > 1:
        cos, sin = jnp.tile(cos, (b, 1)), jnp.tile(sin, (b, 1))
    x2 = x.reshape(n, d)
    q, k, vt, mqk, mv, mo, gt, sga, sgb = _in_proj(x2, cos, sin, in_wts, tb)

    r3 = lambda a: a.reshape(b, s, a.shape[-1])
    bias = jnp.where(jnp.arange(MCHUNK) >= META_PAD, 0.0, NEG).astype(F32)[:, None]
    a_out = _attention(r3(q), k.reshape(n // tb, tb, k.shape[-1]), vt, k_m, vt_m, bias, _pick(s, 512))

    gt3 = r3(gt)
    grow = jnp.swapaxes(gt3[..., :4 * M_HEADS], 1, 2)
    grow_m = jnp.swapaxes(gt_m[:, :4 * M_HEADS], 0, 1)
    h_f, h_b = _mlstm(r3(mqk), r3(mv), gt3, grow, mqk_m, mv_m, gt_m, grow_m, conv_w, conv_b)

    h1 = _merge(x2, a_out.reshape(n, -1), h_f.reshape(n, -1), h_b.reshape(n, -1), mo, sga, sgb, merge_wts, tb)
    return _ffn(h1, ffn_wts, tb).reshape(b, s, d)


def kernel(x_prompt, x_sample, meta_tokens, norm1_g, w_in, b_in, conv_w, conv_b, q_norm_g, kv_norm_g, w_uq, w_ukv, m_norm_g, w_pa, w_pb, w_o, norm2_g, w_ffn_gate, w_ffn_up, w_ffn_down, final_norm_g):
    assert w_in.shape[0] == 1, "single-layer trunk"
    d = x_prompt.shape[-1]
    in_wts = _prep_weights(norm1_g[0], w_in[0], b_in[0], q_norm_g[0], kv_norm_g[0], w_uq[0], w_ukv[0])
    merge_wts = (m_norm_g[0][None, :], w_pa[0].astype(BF16), w_pb[0].astype(BF16), w_o[0].astype(BF16))
    ffn_wts = (norm2_g[0][None, :], w_ffn_gate[0].astype(BF16), w_ffn_up[0].astype(BF16),
               w_ffn_down[0].astype(BF16), final_norm_g[None, :])
    cw, cb = conv_w[0], conv_b[0][None, :]

    hm = jnp.concatenate([jnp.zeros((META_PAD, d), x_prompt.dtype), meta_tokens.astype(x_prompt.dtype)], axis=0)
    cos_m, sin_m = _rope_tables(jnp.arange(MCHUNK) - META_PAD)
    _, k_m, vt_m, mqk_m, mv_m, _, gt_m, _, _ = _in_proj(hm, cos_m, sin_m, in_wts, MCHUNK)
    meta_parts = (k_m, vt_m[0], mqk_m, mv_m, gt_m)

    y_prompt = _trunk(x_prompt, meta_parts, in_wts, cw, cb, merge_wts, ffn_wts)
    y_sample = _trunk(x_sample, meta_parts, in_wts, cw, cb, merge_wts, ffn_wts)
    return (y_prompt, y_sample)
```

```python
import functools

import jax
import jax.numpy as jnp
from jax import lax
from jax.experimental import pallas as pl
from jax.experimental.pallas import tpu as pltpu

F32 = jnp.float32
BF16 = jnp.bfloat16

EPS = 1e-6
N_META = 16
ROPE_THETA = 10000.0
MLA_HEADS = 8
QK_NOPE = 64
QK_ROPE = 32
V_HEAD = 64
QK_DIM = QK_NOPE + QK_ROPE
Q_LORA = 256
KV_LORA = 128
M_HEADS = 4
M_HEAD_DIM = 128
M_WIDTH = M_HEADS * M_HEAD_DIM
LANES = 128
MCHUNK = 128
META_PAD = MCHUNK - N_META
HALO = 16
NEG = -1e30
LOG2E = 1.4426950408889634
GROWTH_LIMIT = 64.0
UNROLL = 2
VMEM_LIMIT = 56 * 1024 * 1024


def _dot(a, b):
    return jnp.dot(a, b, preferred_element_type=F32)


def _dot_nt(a, b):
    return lax.dot_general(a, b, (((1,), (1,)), ((), ())), preferred_element_type=F32)


def _dot_tn(a, b):
    return lax.dot_general(a, b, (((0,), (0,)), ((), ())), preferred_element_type=F32)


def _sigmoid(x):
    return 1.0 / (1.0 + jnp.exp(-x))


def _rms(x, g):
    return x * lax.rsqrt(jnp.mean(x * x, axis=-1, keepdims=True) + EPS) * g


def _const_spec(shape):
    return pl.BlockSpec(shape, lambda *_: (0,) * len(shape))


def _in_proj_kernel(x_ref, cos_ref, sin_ref, n1_ref, w1_ref, b1_ref, wm_ref, bm_ref, wg_ref, bg_ref,
                    qn_ref, kvn_ref, wuq_ref, wuqr_ref, wuk_ref, wuv_ref, vone_ref,
                    q_out, k_out, v_out, mqk_out, mv_out, mo_out, gt_out, sga_out, sgb_out):
    xn = _rms(x_ref[...], n1_ref[...]).astype(BF16)
    cos = cos_ref[...]
    sin = sin_ref[...]

    p1 = _dot(xn, w1_ref[...]) + b1_ref[...]
    c_q = p1[:, :Q_LORA]
    c_kv = p1[:, Q_LORA:Q_LORA + KV_LORA]
    kr_plain = p1[:, Q_LORA + KV_LORA:Q_LORA + KV_LORA + LANES]
    kr_rot = p1[:, Q_LORA + KV_LORA + LANES:]
    k_rope = kr_plain * cos + kr_rot * sin

    cqn = (_rms(c_q, qn_ref[...]) * (QK_DIM ** -0.5 * LOG2E)).astype(BF16)
    qa = _dot(cqn, wuq_ref[...])
    qb = _dot(cqn, wuqr_ref[...])
    ckvn = _rms(c_kv, kvn_ref[...]).astype(BF16)
    ka = _dot(ckvn, wuk_ref[...])
    for h in range(MLA_HEADS):
        sl = slice(h * LANES, (h + 1) * LANES)
        q_out[:, sl] = (qa[:, sl] * cos + qb[:, sl] * sin).astype(BF16)
        k_out[:, sl] = (ka[:, sl] + k_rope).astype(BF16)
    v_out[...] = (_dot_nt(wuv_ref[...], ckvn) + vone_ref[...]).astype(BF16)

    pm = _dot(xn, wm_ref[...]) + bm_ref[...]
    mqk_out[...] = pm[:, :2 * M_WIDTH].astype(BF16)
    mv_out[...] = pm[:, 2 * M_WIDTH:3 * M_WIDTH].astype(BF16)
    mo_out[...] = pm[:, 3 * M_WIDTH:4 * M_WIDTH].astype(BF16)
    gt_out[...] = pm[:, 4 * M_WIDTH:]

    pg = _dot(xn, wg_ref[...]) + bg_ref[...]
    d = sga_out.shape[-1]
    sga_out[...] = _sigmoid(pg[:, :d]).astype(BF16)
    sgb_out[...] = _sigmoid(pg[:, d:]).astype(BF16)


def _in_proj(x2, cos, sin, wts, tb):
    n, d = x2.shape
    widths = (MLA_HEADS * LANES, MLA_HEADS * LANES, MLA_HEADS * LANES, 2 * M_WIDTH, M_WIDTH, M_WIDTH, LANES, d, d)
    dtypes = (BF16, BF16, BF16, BF16, BF16, BF16, F32, BF16, BF16)
    row = lambda w: pl.BlockSpec((tb, w), lambda i: (i, 0))
    out_specs = [row(w) for w in widths]
    out_shape = [jax.ShapeDtypeStruct((n, w), t) for w, t in zip(widths, dtypes)]
    out_specs[2] = pl.BlockSpec((None, widths[2], tb), lambda i: (i, 0, 0))
    out_shape[2] = jax.ShapeDtypeStruct((n // tb, widths[2], tb), BF16)
    return pl.pallas_call(
        _in_proj_kernel,
        grid=(n // tb,),
        in_specs=[row(d), row(LANES), row(LANES)] + [_const_spec(w.shape) for w in wts],
        out_specs=out_specs,
        out_shape=out_shape,
        compiler_params=pltpu.CompilerParams(dimension_semantics=("parallel",), vmem_limit_bytes=VMEM_LIMIT),
        name="in_proj",
    )(x2, cos, sin, *wts)


def _attn_kernel(q_ref, k_ref, vt_ref, km_ref, vtm_ref, bias_ref, o_ref,
                 m_scr, acc_scr, g_scr, mx_scr, al_scr, s_scr, p_scr, *, nk):
    hsl = [slice(0, LANES), slice(LANES, 2 * LANES)]
    qs = [q_ref[:, sl] for sl in hsl]

    def meta_init():
        for hh, sl in enumerate(hsl):
            s = _dot_nt(km_ref[:, sl], qs[hh]) + bias_ref[...]
            m = jnp.max(s, axis=0, keepdims=True)
            p = jnp.exp2(s - m).astype(BF16)
            m_scr[hh] = m
            acc_scr[hh] = _dot(vtm_ref[sl, :], p)

    def pipeline(first, step, n_steps, last):
        first()

        def body(t, carry):
            for u in range(UNROLL):
                step(UNROLL * t + u, u % 2)
            return carry

        n_loop = n_steps // UNROLL
        lax.fori_loop(0, n_loop, body, 0)
        for i in range(n_loop * UNROLL, n_steps):
            step(i, i % 2)
        last()

    def scores_exp(c, slot):
        for hh, sl in enumerate(hsl):
            s = _dot_nt(k_ref[c, :, sl], qs[hh])
            ref = m_scr[hh]
            p_scr[slot, hh] = jnp.exp2(s - ref).astype(BF16)
            m_new = jnp.maximum(ref, jnp.max(s, axis=0, keepdims=True))
            al_scr[slot, hh] = jnp.exp2(ref - m_new)
            g_scr[hh] = jnp.maximum(g_scr[hh], m_new - ref)
            m_scr[hh] = m_new

    def values_rebase(c, slot):
        for hh, sl in enumerate(hsl):
            acc_scr[hh] = (acc_scr[hh] + _dot(vt_ref[c, sl, :], p_scr[slot, hh])) * al_scr[slot, hh]

    def fast_step(i, par):
        scores_exp(i + 1, 1 - par)
        values_rebase(i, par)

    meta_init()
    g_scr[...] = jnp.zeros_like(g_scr)
    pipeline(lambda: scores_exp(0, 0), fast_step, nk - 1, lambda: values_rebase(nk - 1, (nk - 1) % 2))

    def scores(c, slot):
        for hh, sl in enumerate(hsl):
            s = _dot_nt(k_ref[c, :, sl], qs[hh])
            mx_scr[slot, hh] = jnp.max(s, axis=0, keepdims=True)
            s_scr[slot, hh] = s

    def softmax(slot):
        for hh in range(2):
            m_prev = m_scr[hh]
            m_new = jnp.maximum(m_prev, mx_scr[slot, hh])
            al_scr[slot, hh] = jnp.exp2(m_prev - m_new)
            m_scr[hh] = m_new
            p_scr[slot, hh] = jnp.exp2(s_scr[slot, hh] - m_new).astype(BF16)

    def values(c, slot):
        for hh, sl in enumerate(hsl):
            acc_scr[hh] = al_scr[slot, hh] * acc_scr[hh] + _dot(vt_ref[c, sl, :], p_scr[slot, hh])

    def safe_first():
        scores(0, 0)
        scores(1, 1)
        softmax(0)

    def safe_step(i, par):
        scores(i + 2, par)
        softmax(1 - par)
        values(i, par)

    def safe_last():
        softmax(1)
        values(nk - 2, 0)
        values(nk - 1, 1)

    @pl.when(jnp.max(jnp.maximum(g_scr[0], g_scr[1])) > GROWTH_LIMIT)
    def _():
        meta_init()
        pipeline(safe_first, safe_step, nk - 2, safe_last)

    a0 = acc_scr[0]
    a1 = acc_scr[1]
    row = lax.broadcasted_iota(jnp.int32, a0.shape, 0)
    o_t = jnp.where(row < V_HEAD, a0 / a0[V_HEAD:V_HEAD + 1, :], a1 / a1[0:1, :])
    o_ref[...] = o_t.T.astype(BF16)


def _attention(q, k, vt, km, vtm, bias, tq):
    b, s, _ = q.shape
    nk = k.shape[0] // b
    assert nk >= 2 and nk % 2 == 0, "the key-chunk pipeline is unrolled by two"
    tk = k.shape[1]
    hp = MLA_HEADS // 2
    w2 = 2 * LANES
    return pl.pallas_call(
        functools.partial(_attn_kernel, nk=nk),
        grid=(b, hp, s // tq),
        in_specs=[
            pl.BlockSpec((None, tq, w2), lambda bi, h, i: (bi, i, h)),
            pl.BlockSpec((nk, tk, w2), lambda bi, h, i: (bi, 0, h)),
            pl.BlockSpec((nk, w2, tk), lambda bi, h, i: (bi, h, 0)),
            pl.BlockSpec((MCHUNK, w2), lambda bi, h, i: (0, h)),
            pl.BlockSpec((w2, MCHUNK), lambda bi, h, i: (h, 0)),
            pl.BlockSpec((MCHUNK, 1), lambda bi, h, i: (0, 0)),
        ],
        out_specs=pl.BlockSpec((None, tq, LANES), lambda bi, h, i: (bi, i, h)),
        out_shape=jax.ShapeDtypeStruct((b, s, hp * LANES), BF16),
        scratch_shapes=[
            pltpu.VMEM((2, 1, tq), F32), pltpu.VMEM((2, LANES, tq), F32),
            pltpu.VMEM((2, 1, tq), F32),
            pltpu.VMEM((2, 2, 1, tq), F32), pltpu.VMEM((2, 2, 1, tq), F32),
            pltpu.VMEM((2, 2, tk, tq), F32), pltpu.VMEM((2, 2, tk, tq), BF16),
        ],
        compiler_params=pltpu.CompilerParams(
            dimension_semantics=("parallel", "parallel", "arbitrary"), vmem_limit_bytes=VMEM_LIMIT),
        name="mla_attention",
    )(q, k, vt, km, vtm, bias)


def _conv_silu(cur, prev_row, next_row, cw, cb):
    n = cur.shape[0]
    ridx = lax.broadcasted_iota(jnp.int32, cur.shape, 0)
    x_prev = jnp.where(ridx == 0, prev_row, pltpu.roll(cur, 1, axis=0))
    x_next = jnp.where(ridx == n - 1, next_row, pltpu.roll(cur, n - 1, axis=0))
    y = cw[0:1, :] * x_prev + cw[1:2, :] * cur + cw[2:3, :] * x_next + cb
    return y * _sigmoid(y)


def _log_sigmoid(x):
    return jnp.minimum(x, 0.0) - jnp.log(1.0 + jnp.exp(-jnp.abs(x)))


def _tri(n, upper):
    r = lax.broadcasted_iota(jnp.int32, (n, n), 0)
    c = lax.broadcasted_iota(jnp.int32, (n, n), 1)
    return (c >= r) if upper else (c <= r)


def _v_aug(v_h):
    lane = lax.broadcasted_iota(jnp.int32, v_h.shape, 1)
    ones_col = jnp.where(lane == 0, 1.0, 0.0).astype(v_h.dtype)
    return jnp.concatenate([v_h, ones_col], axis=1)


def _state_update(k_h, v_aug, f_col, f_tot, b_col, c_ref, m_ref, h):
    g_col = f_tot - f_col + b_col
    g_max = jnp.max(g_col, axis=0, keepdims=True)
    kw = (k_h * jnp.exp(g_col - g_max)).astype(BF16)
    u = _dot_tn(kw, v_aug)
    m_prev = m_ref[h][0:1, 0:1]
    m_new = jnp.maximum(f_tot + m_prev, g_max)
    c_ref[h] = jnp.exp(f_tot + m_prev - m_new) * c_ref[h] + jnp.exp(g_max - m_new) * u
    m_ref[h] = jnp.broadcast_to(m_new, m_ref.shape[1:])


def _mlstm_chunk(q, k, v, gcol, grow, c_ref, m_ref, reverse):
    n = q.shape[0]
    gi = 2 * M_HEADS if reverse else 0
    gf = gi + M_HEADS
    tri_c = _tri(n, reverse).astype(F32)
    tri_r = _tri(n, not reverse).astype(F32)
    mask = _tri(n, reverse)
    f_cols = jnp.dot(tri_c, _log_sigmoid(gcol), precision=lax.Precision.HIGHEST, preferred_element_type=F32)
    f_rows = jnp.dot(_log_sigmoid(grow), tri_r, precision=lax.Precision.HIGHEST, preferred_element_type=F32)
    last = 0 if reverse else n - 1
    outs = []
    for h in range(M_HEADS):
        sl = slice(h * M_HEAD_DIM, (h + 1) * M_HEAD_DIM)
        f_col = f_cols[:, gf + h:gf + h + 1]
        f_row = f_rows[gf + h:gf + h + 1, :]
        b_col = gcol[:, gi + h:gi + h + 1]
        b_row = grow[gi + h:gi + h + 1, :]
        f_tot = f_col[last:last + 1, :]
        q_h = q[:, sl].astype(BF16)
        k_h = k[:, sl]
        v_aug = _v_aug(v[:, sl])

        log_d = jnp.where(mask, f_col - f_row + b_row, NEG)
        m_loc = jnp.max(log_d, axis=1, keepdims=True)
        s = (_dot_nt(q_h, k_h.astype(BF16)) * jnp.exp(log_d - m_loc)).astype(BF16)
        intra = _dot(s, v_aug)
        inter = _dot(q_h, c_ref[h].astype(BF16))
        m_prev = m_ref[h][0:1, 0:1]
        inter_log = f_col + m_prev
        m_j = jnp.maximum(inter_log, m_loc)
        hh = jnp.exp(inter_log - m_j) * inter + jnp.exp(m_loc - m_j) * intra
        den = hh[:, M_HEAD_DIM:M_HEAD_DIM + 1]
        outs.append(hh[:, :M_HEAD_DIM] / jnp.maximum(jnp.abs(den), jnp.exp(-m_j)))

        _state_update(k_h, v_aug, f_col, f_tot, b_col, c_ref, m_ref, h)
    return jnp.concatenate(outs, axis=1)


def _mlstm_kernel(qk_f, pv_f, nx_f, v_f, gc_f, gr_f, qk_b, pv_b, nx_b, v_b, gc_b, gr_b,
                  qk_m, v_m, gc_m, gr_m, cw_ref, cb_ref, hf_out, hb_out, cf_scr, mf_scr, cb_scr, mb_scr):
    j = pl.program_id(1)
    nc = pl.num_programs(1)
    cw = cw_ref[...]
    cb = cb_ref[...]
    scale = M_HEAD_DIM ** -0.5
    meta_last = qk_m[MCHUNK - 1:MCHUNK, :].astype(F32)

    @pl.when(j == 0)
    def _():
        cf_scr[...] = jnp.zeros_like(cf_scr)
        mf_scr[...] = jnp.zeros_like(mf_scr)
        cb_scr[...] = jnp.zeros_like(cb_scr)
        mb_scr[...] = jnp.zeros_like(mb_scr)
        ridx = lax.broadcasted_iota(jnp.int32, (MCHUNK, 1), 0)
        valid = ridx >= META_PAD
        cur = jnp.where(valid, qk_m[...].astype(F32), 0.0)
        first_x = qk_f[0:1, :].astype(F32)
        k_all = _conv_silu(cur, jnp.zeros_like(first_x), first_x, cw, cb)[:, M_WIDTH:] * scale
        gcol = gc_m[...]
        a_col = jnp.where(valid, _log_sigmoid(gcol), 0.0)
        f_cols = jnp.dot(_tri(MCHUNK, False).astype(F32), a_col, precision=lax.Precision.HIGHEST,
                         preferred_element_type=F32)
        for h in range(M_HEADS):
            sl = slice(h * M_HEAD_DIM, (h + 1) * M_HEAD_DIM)
            f_col = f_cols[:, M_HEADS + h:M_HEADS + h + 1]
            b_col = jnp.where(valid, gcol[:, h:h + 1], NEG)
            _state_update(k_all[:, sl], _v_aug(v_m[:, sl]), f_col, f_col[MCHUNK - 1:MCHUNK, :], b_col,
                          cf_scr, mf_scr, h)

    def run(qk_ref, pv_ref, nx_ref, v_ref, gc_ref, gr_ref, c, c_scr, m_scr, reverse, out_ref):
        prev_row = jnp.where(c == 0, meta_last, pv_ref[HALO - 1:HALO, :].astype(F32))
        next_row = jnp.where(c == nc - 1, 0.0, nx_ref[0:1, :].astype(F32))
        qk = _conv_silu(qk_ref[...].astype(F32), prev_row, next_row, cw, cb)
        h = _mlstm_chunk(qk[:, :M_WIDTH], qk[:, M_WIDTH:] * scale, v_ref[...], gc_ref[...], gr_ref[...],
                         c_scr, m_scr, reverse)
        out_ref[...] = h.astype(out_ref.dtype)

    run(qk_f, pv_f, nx_f, v_f, gc_f, gr_f, j, cf_scr, mf_scr, False, hf_out)
    run(qk_b, pv_b, nx_b, v_b, gc_b, gr_b, nc - 1 - j, cb_scr, mb_scr, True, hb_out)


def _mlstm(mqk, mv, gcol, grow, mqk_m, mv_m, gcol_m, grow_m, conv_w, conv_b):
    b, s, _ = mqk.shape
    nc = s // MCHUNK
    hpc = MCHUNK // HALO
    nhb = s // HALO

    def specs(chunk_of):
        return [
            pl.BlockSpec((None, MCHUNK, 2 * M_WIDTH), lambda bi, j: (bi, chunk_of(j), 0)),
            pl.BlockSpec((None, HALO, 2 * M_WIDTH), lambda bi, j: (bi, jnp.maximum(chunk_of(j) * hpc - 1, 0), 0)),
            pl.BlockSpec((None, HALO, 2 * M_WIDTH),
                         lambda bi, j: (bi, jnp.minimum((chunk_of(j) + 1) * hpc, nhb - 1), 0)),
            pl.BlockSpec((None, MCHUNK, M_WIDTH), lambda bi, j: (bi, chunk_of(j), 0)),
            pl.BlockSpec((None, MCHUNK, LANES), lambda bi, j: (bi, chunk_of(j), 0)),
            pl.BlockSpec((None, 4 * M_HEADS, MCHUNK), lambda bi, j: (bi, 0, chunk_of(j))),
        ]

    fwd = lambda j: j
    bwd = lambda j: nc - 1 - j
    operands = [mqk, mqk, mqk, mv, gcol, grow]
    out_spec = lambda chunk_of: pl.BlockSpec((None, MCHUNK, M_WIDTH), lambda bi, j: (bi, chunk_of(j), 0))
    return pl.pallas_call(
        _mlstm_kernel,
        grid=(b, nc),
        in_specs=specs(fwd) + specs(bwd) + [_const_spec(a.shape) for a in (mqk_m, mv_m, gcol_m, grow_m, conv_w, conv_b)],
        out_specs=[out_spec(fwd), out_spec(bwd)],
        out_shape=[jax.ShapeDtypeStruct((b, s, M_WIDTH), BF16)] * 2,
        scratch_shapes=[
            pltpu.VMEM((M_HEADS, M_HEAD_DIM, 2 * M_HEAD_DIM), F32), pltpu.VMEM((M_HEADS, 8, LANES), F32),
            pltpu.VMEM((M_HEADS, M_HEAD_DIM, 2 * M_HEAD_DIM), F32), pltpu.VMEM((M_HEADS, 8, LANES), F32),
        ],
        compiler_params=pltpu.CompilerParams(
            dimension_semantics=("parallel", "arbitrary"), vmem_limit_bytes=VMEM_LIMIT),
        name="mlstm",
    )(*operands, *operands, mqk_m, mv_m, gcol_m, grow_m, conv_w, conv_b)


def _merge_kernel(x_ref, a_ref, hf_ref, hb_ref, mo_ref, sga_ref, sgb_ref, mg_ref, wpa_ref, wpb_ref, wo_ref, o_ref):
    h = (hf_ref[...].astype(F32) + hb_ref[...].astype(F32)) * _sigmoid(mo_ref[...].astype(F32))
    mg = mg_ref[...]
    parts = []
    for hd in range(M_HEADS):
        sl = slice(hd * M_HEAD_DIM, (hd + 1) * M_HEAD_DIM)
        parts.append(_rms(h[:, sl], mg[:, sl]).astype(BF16))
    m_out = jnp.concatenate(parts, axis=1)
    merged = (sga_ref[...].astype(F32) * _dot(a_ref[...], wpa_ref[...])
              + sgb_ref[...].astype(F32) * _dot(m_out, wpb_ref[...]))
    o_ref[...] = x_ref[...] + _dot(merged.astype(BF16), wo_ref[...])


def _merge(x2, a2, hf2, hb2, mo2, sga2, sgb2, wts, tb):
    n, d = x2.shape
    row = lambda w: pl.BlockSpec((tb, w), lambda i: (i, 0))
    return pl.pallas_call(
        _merge_kernel,
        grid=(n // tb,),
        in_specs=[row(d), row(a2.shape[1]), row(M_WIDTH), row(M_WIDTH), row(M_WIDTH), row(d), row(d)]
        + [_const_spec(w.shape) for w in wts],
        out_specs=row(d),
        out_shape=jax.ShapeDtypeStruct((n, d), F32),
        compiler_params=pltpu.CompilerParams(dimension_semantics=("parallel",), vmem_limit_bytes=VMEM_LIMIT),
        name="merge",
    )(x2, a2, hf2, hb2, mo2, sga2, sgb2, *wts)


def _ffn_kernel(x_ref, n2_ref, wg_ref, wu_ref, wd_ref, fn_ref, o_ref):
    x = x_ref[...]
    xn = _rms(x, n2_ref[...]).astype(BF16)
    g = _dot(xn, wg_ref[...])
    u = _dot(xn, wu_ref[...])
    act = (g * _sigmoid(g) * u).astype(BF16)
    y = x + _dot(act, wd_ref[...])
    o_ref[...] = _rms(y, fn_ref[...])


def _ffn(x2, wts, tb):
    n, d = x2.shape
    row = pl.BlockSpec((tb, d), lambda i: (i, 0))
    return pl.pallas_call(
        _ffn_kernel,
        grid=(n // tb,),
        in_specs=[row] + [_const_spec(w.shape) for w in wts],
        out_specs=row,
        out_shape=jax.ShapeDtypeStruct((n, d), F32),
        compiler_params=pltpu.CompilerParams(dimension_semantics=("parallel",), vmem_limit_bytes=VMEM_LIMIT),
        name="ffn",
    )(x2, *wts)


def _rope_tables(pos):
    half = QK_ROPE // 2
    freqs = ROPE_THETA ** (-jnp.arange(half, dtype=F32) / half)
    ang = pos.astype(F32)[:, None] * freqs[None, :]
    c, s = jnp.cos(ang), jnp.sin(ang)
    n = pos.shape[0]
    pad = jnp.zeros((n, LANES - QK_DIM), F32)
    cos = jnp.concatenate([jnp.ones((n, QK_NOPE), F32), c, c, pad], axis=1)
    sin = jnp.concatenate([jnp.zeros((n, QK_NOPE), F32), s, s, pad], axis=1)
    return cos, sin


def _rot(w):
    half = QK_ROPE // 2
    return jnp.concatenate([-w[..., half:], w[..., :half]], axis=-1)


def _pad_cols(w, left, total):
    return jnp.pad(w, [(0, 0)] * (w.ndim - 1) + [(left, total - left - w.shape[-1])])


def _prep_weights(norm1_g, w_in, b_in, q_norm_g, kv_norm_g, w_uq, w_ukv):
    d = w_in.shape[0]
    o_kr = Q_LORA + KV_LORA
    o_m = o_kr + QK_ROPE
    o_g = o_m + 4 * M_WIDTH
    o_ga = o_g + 4 * M_HEADS

    def seg1(w):
        kr = w[..., o_kr:o_m]
        return jnp.concatenate([w[..., :o_kr], _pad_cols(kr, QK_NOPE, LANES), _pad_cols(_rot(kr), QK_NOPE, LANES)], axis=-1)

    def segm(w):
        return jnp.concatenate([w[..., o_m:o_g], _pad_cols(w[..., o_g:o_ga], 0, LANES)], axis=-1)

    b2 = b_in[None, :]
    w1, b1 = seg1(w_in).astype(BF16), seg1(b2)
    wm, bm = segm(w_in).astype(BF16), segm(b2)
    wg, bg = w_in[:, o_ga:].astype(BF16), b2[:, o_ga:]

    uq = w_uq.reshape(Q_LORA, MLA_HEADS, QK_DIM)
    wuq = _pad_cols(uq, 0, LANES).reshape(Q_LORA, MLA_HEADS * LANES).astype(BF16)
    wuqr = _pad_cols(_rot(uq[..., QK_NOPE:]), QK_NOPE, LANES).reshape(Q_LORA, MLA_HEADS * LANES).astype(BF16)
    ukv = w_ukv.reshape(KV_LORA, MLA_HEADS, QK_NOPE + V_HEAD)
    wuk = _pad_cols(ukv[..., :QK_NOPE], 0, LANES).reshape(KV_LORA, MLA_HEADS * LANES).astype(BF16)
    uv = ukv[..., QK_NOPE:].reshape(KV_LORA, MLA_HEADS // 2, 2, V_HEAD)
    wuv = jnp.stack([_pad_cols(uv[:, :, 0], 0, LANES), _pad_cols(uv[:, :, 1], V_HEAD, LANES)], axis=2)
    wuv = wuv.reshape(KV_LORA, MLA_HEADS * LANES).T.astype(BF16)
    lane = jnp.arange(2 * LANES)
    vone = jnp.tile(((lane == V_HEAD) | (lane == LANES)).astype(F32), MLA_HEADS // 2)[:, None]
    return (norm1_g[None, :], w1, b1, wm, bm, wg, bg, q_norm_g[None, :], kv_norm_g[None, :], wuq, wuqr, wuk, wuv, vone)


def _pick(n, pref):
    return pref if n % pref == 0 else n


def _trunk(x, meta_parts, in_wts, conv_w, conv_b, merge_wts, ffn_wts):
    b, s, d = x.shape
    n = b * s
    tb = _pick(n, 512)
    k_m, vt_m, mqk_m, mv_m, gt_m = meta_parts

    cos, sin = _rope_tables(N_META + jnp.arange(s))
    if b > 1:
        cos, sin = jnp.tile(cos, (b, 1)), jnp.tile(sin, (b, 1))
    x2 = x.reshape(n, d)
    q, k, vt, mqk, mv, mo, gt, sga, sgb = _in_proj(x2, cos, sin, in_wts, tb)

    r3 = lambda a: a.reshape(b, s, a.shape[-1])
    bias = jnp.where(jnp.arange(MCHUNK) >= META_PAD, 0.0, NEG).astype(F32)[:, None]
    a_out = _attention(r3(q), k.reshape(n // tb, tb, k.shape[-1]), vt, k_m, vt_m, bias, _pick(s, 512))

    gt3 = r3(gt)
    grow = jnp.swapaxes(gt3[..., :4 * M_HEADS], 1, 2)
    grow_m = jnp.swapaxes(gt_m[:, :4 * M_HEADS], 0, 1)
    h_f, h_b = _mlstm(r3(mqk), r3(mv), gt3, grow, mqk_m, mv_m, gt_m, grow_m, conv_w, conv_b)

    h1 = _merge(x2, a_out.reshape(n, -1), h_f.reshape(n, -1), h_b.reshape(n, -1), mo, sga, sgb, merge_wts, tb)
    return _ffn(h1, ffn_wts, tb).reshape(b, s, d)


def kernel(x_prompt, x_sample, meta_tokens, norm1_g, w_in, b_in, conv_w, conv_b, q_norm_g, kv_norm_g, w_uq, w_ukv, m_norm_g, w_pa, w_pb, w_o, norm2_g, w_ffn_gate, w_ffn_up, w_ffn_down, final_norm_g):
    assert w_in.shape[0] == 1, "single-layer trunk"
    d = x_prompt.shape[-1]
    in_wts = _prep_weights(norm1_g[0], w_in[0], b_in[0], q_norm_g[0], kv_norm_g[0], w_uq[0], w_ukv[0])
    merge_wts = (m_norm_g[0][None, :], w_pa[0].astype(BF16), w_pb[0].astype(BF16), w_o[0].astype(BF16))
    ffn_wts = (norm2_g[0][None, :], w_ffn_gate[0].astype(BF16), w_ffn_up[0].astype(BF16),
               w_ffn_down[0].astype(BF16), final_norm_g[None, :])
    cw, cb = conv_w[0], conv_b[0][None, :]

    hm = jnp.concatenate([jnp.zeros((META_PAD, d), x_prompt.dtype), meta_tokens.astype(x_prompt.dtype)], axis=0)
    cos_m, sin_m = _rope_tables(jnp.arange(MCHUNK) - META_PAD)
    _, k_m, vt_m, mqk_m, mv_m, _, gt_m, _, _ = _in_proj(hm, cos_m, sin_m, in_wts, MCHUNK)
    meta_parts = (k_m, vt_m[0], mqk_m, mv_m, gt_m)

    y_prompt = _trunk(x_prompt, meta_parts, in_wts, cw, cb, merge_wts, ffn_wts)
    y_sample = _trunk(x_sample, meta_parts, in_wts, cw, cb, merge_wts, ffn_wts)
    return (y_prompt, y_sample)
```

```python
import functools

import jax
import jax.numpy as jnp
from jax import lax
from jax.experimental import pallas as pl
from jax.experimental.pallas import tpu as pltpu

F32 = jnp.float32
BF16 = jnp.bfloat16

EPS = 1e-6
N_META = 16
ROPE_THETA = 10000.0
MLA_HEADS = 8
QK_NOPE = 64
QK_ROPE = 32
V_HEAD = 64
QK_DIM = QK_NOPE + QK_ROPE
Q_LORA = 256
KV_LORA = 128
M_HEADS = 4
M_HEAD_DIM = 128
M_WIDTH = M_HEADS * M_HEAD_DIM
LANES = 128
MCHUNK = 128
META_PAD = MCHUNK - N_META
HALO = 16
NEG = -1e30
LOG2E = 1.4426950408889634
GROWTH_LIMIT = 64.0
FAST_UNROLL = 16
VMEM_LIMIT = 56 * 1024 * 1024


def _dot(a, b):
    return jnp.dot(a, b, preferred_element_type=F32)


def _dot_nt(a, b):
    return lax.dot_general(a, b, (((1,), (1,)), ((), ())), preferred_element_type=F32)


def _dot_tn(a, b):
    return lax.dot_general(a, b, (((0,), (0,)), ((), ())), preferred_element_type=F32)


def _sigmoid(x):
    return 1.0 / (1.0 + jnp.exp(-x))


def _rms(x, g):
    return x * lax.rsqrt(jnp.mean(x * x, axis=-1, keepdims=True) + EPS) * g


def _const_spec(shape):
    return pl.BlockSpec(shape, lambda *_: (0,) * len(shape))


def _in_proj_kernel(x_ref, cos_ref, sin_ref, n1_ref, w1_ref, b1_ref, wm_ref, bm_ref, wg_ref, bg_ref,
                    qn_ref, kvn_ref, wuq_ref, wuqr_ref, wuk_ref, wuv_ref, vone_ref, wmv_ref, bmv_ref,
                    q_out, k_out, v_out, mqk_out, mv_out, mo_out, gt_out, sga_out, sgb_out):
    xn = _rms(x_ref[...], n1_ref[...]).astype(BF16)
    cos = cos_ref[...]
    sin = sin_ref[...]

    p1 = _dot(xn, w1_ref[...]) + b1_ref[...]
    c_q = p1[:, :Q_LORA]
    c_kv = p1[:, Q_LORA:Q_LORA + KV_LORA]
    kr_plain = p1[:, Q_LORA + KV_LORA:Q_LORA + KV_LORA + LANES]
    kr_rot = p1[:, Q_LORA + KV_LORA + LANES:]
    k_rope = kr_plain * cos + kr_rot * sin

    cqn = (_rms(c_q, qn_ref[...]) * (QK_DIM ** -0.5 * LOG2E)).astype(BF16)
    qa = _dot(cqn, wuq_ref[...])
    qb = _dot(cqn, wuqr_ref[...])
    ckvn = _rms(c_kv, kvn_ref[...]).astype(BF16)
    ka = _dot(ckvn, wuk_ref[...])
    for h in range(MLA_HEADS):
        sl = slice(h * LANES, (h + 1) * LANES)
        q_out[:, sl] = (qa[:, sl] * cos + qb[:, sl] * sin).astype(BF16)
        k_out[:, sl] = (ka[:, sl] + k_rope).astype(BF16)
    v_out[...] = (_dot_nt(wuv_ref[...], ckvn) + vone_ref[...]).astype(BF16)

    pm = _dot(xn, wm_ref[...]) + bm_ref[...]
    mqk_out[...] = pm[:, :2 * M_WIDTH].astype(BF16)
    mo_out[...] = pm[:, 2 * M_WIDTH:3 * M_WIDTH].astype(BF16)
    gt_out[...] = pm[:, 3 * M_WIDTH:]
    mv_out[...] = (_dot_nt(wmv_ref[...], xn) + bmv_ref[...]).astype(BF16)

    pg = _dot(xn, wg_ref[...]) + bg_ref[...]
    d = sga_out.shape[-1]
    sga_out[...] = _sigmoid(pg[:, :d]).astype(BF16)
    sgb_out[...] = _sigmoid(pg[:, d:]).astype(BF16)


def _in_proj(x2, cos, sin, wts, tb, seq):
    n, d = x2.shape
    rope = pl.BlockSpec((tb, LANES), lambda i: (i % (seq // tb), 0))
    widths = (MLA_HEADS * LANES, MLA_HEADS * LANES, MLA_HEADS * LANES, 2 * M_WIDTH, M_WIDTH, M_WIDTH, 2 * LANES, d, d)
    dtypes = (BF16, BF16, BF16, BF16, BF16, BF16, F32, BF16, BF16)
    row = lambda w: pl.BlockSpec((tb, w), lambda i: (i, 0))
    out_specs = [row(w) for w in widths]
    out_shape = [jax.ShapeDtypeStruct((n, w), t) for w, t in zip(widths, dtypes)]
    for o in (2, 4):
        out_specs[o] = pl.BlockSpec((None, widths[o], tb), lambda i: (i, 0, 0))
        out_shape[o] = jax.ShapeDtypeStruct((n // tb, widths[o], tb), BF16)
    return pl.pallas_call(
        _in_proj_kernel,
        grid=(n // tb,),
        in_specs=[row(d), rope, rope] + [_const_spec(w.shape) for w in wts],
        out_specs=out_specs,
        out_shape=out_shape,
        compiler_params=pltpu.CompilerParams(dimension_semantics=("parallel",), vmem_limit_bytes=VMEM_LIMIT),
        name="in_proj",
    )(x2, cos, sin, *wts)


def _attn_kernel(q_ref, k_ref, vt_ref, km_ref, vtm_ref, bias_ref, o_ref,
                 m_scr, acc_scr, g_scr, mx_scr, al_scr, s_scr, p_scr, *, nk):
    hsl = [slice(0, LANES), slice(LANES, 2 * LANES)]
    qs = [q_ref[:, sl] for sl in hsl]

    def meta_init():
        for hh, sl in enumerate(hsl):
            s = _dot_nt(km_ref[:, sl], qs[hh]) + bias_ref[...]
            m = jnp.max(s, axis=0, keepdims=True)
            p = jnp.exp2(s - m).astype(BF16)
            m_scr[hh] = m
            acc_scr[hh] = _dot(vtm_ref[sl, :], p)

    def pipeline(first, step, n_steps, last, unroll):
        first()

        def body(t, carry):
            for u in range(unroll):
                step(unroll * t + u, u % 2)
            return carry

        n_loop = n_steps // unroll
        lax.fori_loop(0, n_loop, body, 0)
        for i in range(n_loop * unroll, n_steps):
            step(i, i % 2)
        last()

    def scores_exp(c, slot):
        for hh, sl in enumerate(hsl):
            s = _dot_nt(k_ref[c, :, sl], qs[hh])
            ref = m_scr[hh]
            p_scr[slot, hh] = jnp.exp2(s - ref).astype(BF16)
            m_new = jnp.maximum(ref, jnp.max(s, axis=0, keepdims=True))
            al_scr[slot, hh] = jnp.exp2(ref - m_new)
            g_scr[hh] = jnp.maximum(g_scr[hh], m_new - ref)
            m_scr[hh] = m_new

    def values_rebase(c, slot):
        for hh, sl in enumerate(hsl):
            acc_scr[hh] = (acc_scr[hh] + _dot(vt_ref[c, sl, :], p_scr[slot, hh])) * al_scr[slot, hh]

    def fast_step(i, par):
        scores_exp(i + 1, 1 - par)
        values_rebase(i, par)

    meta_init()
    g_scr[...] = jnp.zeros_like(g_scr)
    pipeline(lambda: scores_exp(0, 0), fast_step, nk - 1, lambda: values_rebase(nk - 1, (nk - 1) % 2), FAST_UNROLL)

    def scores(c, slot):
        for hh, sl in enumerate(hsl):
            s = _dot_nt(k_ref[c, :, sl], qs[hh])
            mx_scr[slot, hh] = jnp.max(s, axis=0, keepdims=True)
            s_scr[slot, hh] = s

    def softmax(slot):
        for hh in range(2):
            m_prev = m_scr[hh]
            m_new = jnp.maximum(m_prev, mx_scr[slot, hh])
            al_scr[slot, hh] = jnp.exp2(m_prev - m_new)
            m_scr[hh] = m_new
            p_scr[slot, hh] = jnp.exp2(s_scr[slot, hh] - m_new).astype(BF16)

    def values(c, slot):
        for hh, sl in enumerate(hsl):
            acc_scr[hh] = al_scr[slot, hh] * acc_scr[hh] + _dot(vt_ref[c, sl, :], p_scr[slot, hh])

    def safe_first():
        scores(0, 0)
        scores(1, 1)
        softmax(0)

    def safe_step(i, par):
        scores(i + 2, par)
        softmax(1 - par)
        values(i, par)

    def safe_last():
        softmax(1)
        values(nk - 2, 0)
        values(nk - 1, 1)

    @pl.when(jnp.max(jnp.maximum(g_scr[0], g_scr[1])) > GROWTH_LIMIT)
    def _():
        meta_init()
        pipeline(safe_first, safe_step, nk - 2, safe_last, 2)

    a0 = acc_scr[0]
    a1 = acc_scr[1]
    row = lax.broadcasted_iota(jnp.int32, a0.shape, 0)
    o_t = jnp.where(row < V_HEAD, a0 / a0[V_HEAD:V_HEAD + 1, :], a1 / a1[0:1, :])
    o_ref[...] = o_t.T.astype(BF16)


def _attention(q, k, vt, km, vtm, bias, tq):
    b, s, _ = q.shape
    nk = k.shape[0] // b
    assert nk >= 2 and nk % 2 == 0, "the key-chunk pipeline is unrolled by two"
    tk = k.shape[1]
    hp = MLA_HEADS // 2
    w2 = 2 * LANES
    return pl.pallas_call(
        functools.partial(_attn_kernel, nk=nk),
        grid=(b, hp, s // tq),
        in_specs=[
            pl.BlockSpec((None, tq, w2), lambda bi, h, i: (bi, i, h)),
            pl.BlockSpec((nk, tk, w2), lambda bi, h, i: (bi, 0, h)),
            pl.BlockSpec((nk, w2, tk), lambda bi, h, i: (bi, h, 0)),
            pl.BlockSpec((MCHUNK, w2), lambda bi, h, i: (0, h)),
            pl.BlockSpec((w2, MCHUNK), lambda bi, h, i: (h, 0)),
            pl.BlockSpec((MCHUNK, 1), lambda bi, h, i: (0, 0)),
        ],
        out_specs=pl.BlockSpec((None, tq, LANES), lambda bi, h, i: (bi, i, h)),
        out_shape=jax.ShapeDtypeStruct((b, s, hp * LANES), BF16),
        scratch_shapes=[
            pltpu.VMEM((2, 1, tq), F32), pltpu.VMEM((2, LANES, tq), F32),
            pltpu.VMEM((2, 1, tq), F32),
            pltpu.VMEM((2, 2, 1, tq), F32), pltpu.VMEM((2, 2, 1, tq), F32),
            pltpu.VMEM((2, 2, tk, tq), F32), pltpu.VMEM((2, 2, tk, tq), BF16),
        ],
        compiler_params=pltpu.CompilerParams(
            dimension_semantics=("parallel", "parallel", "arbitrary"), vmem_limit_bytes=VMEM_LIMIT),
        name="mla_attention",
    )(q, k, vt, km, vtm, bias)


def _conv_silu(cur, prev_row, next_row, cw, cb):
    n = cur.shape[0]
    ridx = lax.broadcasted_iota(jnp.int32, cur.shape, 0)
    x_prev = jnp.where(ridx == 0, prev_row, pltpu.roll(cur, 1, axis=0))
    x_next = jnp.where(ridx == n - 1, next_row, pltpu.roll(cur, n - 1, axis=0))
    y = cw[0:1, :] * x_prev + cw[1:2, :] * cur + cw[2:3, :] * x_next + cb
    return y * _sigmoid(y)


def _log_sigmoid(x):
    return jnp.minimum(x, 0.0) - jnp.log(1.0 + jnp.exp(-jnp.abs(x)))


def _tri(n, upper):
    r = lax.broadcasted_iota(jnp.int32, (n, n), 0)
    c = lax.broadcasted_iota(jnp.int32, (n, n), 1)
    return (c >= r) if upper else (c <= r)


AUG = 16
CROWS = M_HEAD_DIM + AUG


def _v_aug_t(vt_h):
    r = lax.broadcasted_iota(jnp.int32, (AUG, vt_h.shape[1]), 0)
    return jnp.concatenate([vt_h, jnp.where(r == 0, 1.0, 0.0).astype(vt_h.dtype)], axis=0)


def _gate_rows(gi8, gf8, valid, tri_t, last, m_prev8):
    a8 = _log_sigmoid(gf8)
    b8 = gi8
    if valid is not None:
        a8 = jnp.where(valid, a8, 0.0)
        b8 = jnp.where(valid, b8, NEG)
    f8 = jnp.dot(a8, tri_t, precision=lax.Precision.HIGHEST, preferred_element_type=F32)
    f_tot = f8[:, last:last + 1]
    g8 = f_tot - f8 + b8
    m_new8 = jnp.maximum(f_tot + m_prev8, jnp.max(g8, axis=1, keepdims=True))
    return f8, jnp.exp(g8 - m_new8), jnp.exp(f_tot + m_prev8 - m_new8), m_new8


def _state_update(k_h, vaug_t, w_row, decay, ct_ref, h):
    vw = (vaug_t.astype(F32) * w_row).astype(BF16)
    ct_ref[h] = decay * ct_ref[h] + _dot(vw, k_h)


def _mlstm_chunk(q, k, vt_ref, gc, gr, ct_ref, m_ref, reverse):
    n = q.shape[0]
    r0 = M_HEADS if reverse else 0
    mask_t = _tri(n, not reverse)
    tri_t = mask_t.astype(F32)
    tri_c = _tri(n, reverse).astype(F32)
    last = 0 if reverse else n - 1
    m_prev8 = m_ref[:, 0:1]
    f8, w8, decay8, m_new8 = _gate_rows(gr[0:8, :], gr[8:16, :], None, tri_t, last, m_prev8)
    inter8 = f8 + m_prev8
    c_cols = jnp.dot(tri_c, _log_sigmoid(gc[:, LANES:]), precision=lax.Precision.HIGHEST,
                     preferred_element_type=F32) - gc[:, :LANES]
    outs = []
    for h in range(M_HEADS):
        r = r0 + h
        sl = slice(h * M_HEAD_DIM, (h + 1) * M_HEAD_DIM)
        q_h = q[:, sl].astype(BF16)
        k_h = k[:, sl].astype(BF16)
        vaug_t = _v_aug_t(vt_ref[sl, :])
        inter = inter8[r:r + 1, :]
        log_d = jnp.where(mask_t, f8[r:r + 1, :] - c_cols[:, r:r + 1], NEG)
        m_j = jnp.maximum(inter, jnp.max(log_d, axis=0, keepdims=True))
        s_t = (_dot_nt(k_h, q_h) * jnp.exp(log_d - m_j)).astype(BF16)
        h_t = _dot_nt(ct_ref[h].astype(BF16), q_h) * jnp.exp(inter - m_j) + _dot(vaug_t, s_t)
        den = h_t[M_HEAD_DIM:M_HEAD_DIM + 1, :]
        out_t = h_t[:M_HEAD_DIM, :] / jnp.maximum(jnp.abs(den), jnp.exp(-m_j))
        outs.append(out_t.T)
        _state_update(k_h, vaug_t, w8[r:r + 1, :], decay8[r:r + 1, :], ct_ref, h)
    m_ref[...] = jnp.broadcast_to(m_new8, m_ref.shape)
    return jnp.concatenate(outs, axis=1)


def _mlstm_kernel(qk_f, pv_f, nx_f, vt_f, gc_f, gr_f, qk_b, pv_b, nx_b, vt_b, gc_b, gr_b,
                  qk_m, vt_m, gr_m, cw_ref, cb_ref, hf_out, hb_out, cf_scr, mf_scr, cb_scr, mb_scr):
    j = pl.program_id(1)
    nc = pl.num_programs(1)
    cw = cw_ref[...]
    cb = cb_ref[...]
    scale = M_HEAD_DIM ** -0.5
    meta_last = qk_m[MCHUNK - 1:MCHUNK, :].astype(F32)

    @pl.when(j == 0)
    def _():
        cf_scr[...] = jnp.zeros_like(cf_scr)
        cb_scr[...] = jnp.zeros_like(cb_scr)
        mb_scr[...] = jnp.zeros_like(mb_scr)
        valid_col = lax.broadcasted_iota(jnp.int32, (MCHUNK, 1), 0) >= META_PAD
        valid_row = lax.broadcasted_iota(jnp.int32, (1, MCHUNK), 1) >= META_PAD
        cur = jnp.where(valid_col, qk_m[...].astype(F32), 0.0)
        first_x = qk_f[0:1, :].astype(F32)
        k_all = _conv_silu(cur, jnp.zeros_like(first_x), first_x, cw, cb)[:, M_WIDTH:] * scale
        _, w8, decay8, m_new8 = _gate_rows(gr_m[0:8, :], gr_m[8:16, :], valid_row, _tri(MCHUNK, True).astype(F32),
                                           MCHUNK - 1, jnp.zeros((8, 1), F32))
        for h in range(M_HEADS):
            sl = slice(h * M_HEAD_DIM, (h + 1) * M_HEAD_DIM)
            _state_update(k_all[:, sl].astype(BF16), _v_aug_t(vt_m[sl, :]), w8[h:h + 1, :], decay8[h:h + 1, :],
                          cf_scr, h)
        mf_scr[...] = jnp.broadcast_to(m_new8, mf_scr.shape)

    def run(qk_ref, pv_ref, nx_ref, vt_ref, gc_ref, gr_ref, c, c_scr, m_scr, reverse, out_ref):
        prev_row = jnp.where(c == 0, meta_last, pv_ref[HALO - 1:HALO, :].astype(F32))
        next_row = jnp.where(c == nc - 1, 0.0, nx_ref[0:1, :].astype(F32))
        qk = _conv_silu(qk_ref[...].astype(F32), prev_row, next_row, cw, cb)
        h = _mlstm_chunk(qk[:, :M_WIDTH], qk[:, M_WIDTH:] * scale, vt_ref, gc_ref[...], gr_ref[...],
                         c_scr, m_scr, reverse)
        out_ref[...] = h.astype(out_ref.dtype)

    run(qk_f, pv_f, nx_f, vt_f, gc_f, gr_f, j, cf_scr, mf_scr, False, hf_out)
    run(qk_b, pv_b, nx_b, vt_b, gc_b, gr_b, nc - 1 - j, cb_scr, mb_scr, True, hb_out)


def _mlstm(mqk, mvt, gcol, grow, mqk_m, mvt_m, grow_m, conv_w, conv_b):
    b, s, _ = mqk.shape
    nc = s // MCHUNK
    hpc = MCHUNK // HALO
    nhb = s // HALO
    cps = mvt.shape[2] // MCHUNK
    spb = mvt.shape[0] // b

    def specs(chunk_of):
        return [
            pl.BlockSpec((None, MCHUNK, 2 * M_WIDTH), lambda bi, j: (bi, chunk_of(j), 0)),
            pl.BlockSpec((None, HALO, 2 * M_WIDTH), lambda bi, j: (bi, jnp.maximum(chunk_of(j) * hpc - 1, 0), 0)),
            pl.BlockSpec((None, HALO, 2 * M_WIDTH),
                         lambda bi, j: (bi, jnp.minimum((chunk_of(j) + 1) * hpc, nhb - 1), 0)),
            pl.BlockSpec((None, M_WIDTH, MCHUNK), lambda bi, j: (bi * spb + chunk_of(j) // cps, 0, chunk_of(j) % cps)),
            pl.BlockSpec((None, MCHUNK, 2 * LANES), lambda bi, j: (bi, chunk_of(j), 0)),
            pl.BlockSpec((None, 4 * M_HEADS, MCHUNK), lambda bi, j: (bi, 0, chunk_of(j))),
        ]

    fwd = lambda j: j
    bwd = lambda j: nc - 1 - j
    operands = [mqk, mqk, mqk, mvt, gcol, grow]
    out_spec = lambda chunk_of: pl.BlockSpec((None, MCHUNK, M_WIDTH), lambda bi, j: (bi, chunk_of(j), 0))
    state = [pltpu.VMEM((M_HEADS, CROWS, M_HEAD_DIM), F32), pltpu.VMEM((2 * M_HEADS, LANES), F32)]
    return pl.pallas_call(
        _mlstm_kernel,
        grid=(b, nc),
        in_specs=specs(fwd) + specs(bwd) + [_const_spec(a.shape) for a in (mqk_m, mvt_m, grow_m, conv_w, conv_b)],
        out_specs=[out_spec(fwd), out_spec(bwd)],
        out_shape=[jax.ShapeDtypeStruct((b, s, M_WIDTH), BF16)] * 2,
        scratch_shapes=state + state,
        compiler_params=pltpu.CompilerParams(
            dimension_semantics=("parallel", "arbitrary"), vmem_limit_bytes=VMEM_LIMIT),
        name="mlstm",
    )(*operands, *operands, mqk_m, mvt_m, grow_m, conv_w, conv_b)


def _merge_kernel(x_ref, a_ref, hf_ref, hb_ref, mo_ref, sga_ref, sgb_ref, mg_ref, wpa_ref, wpb_ref, wo_ref, o_ref):
    h = (hf_ref[...].astype(F32) + hb_ref[...].astype(F32)) * _sigmoid(mo_ref[...].astype(F32))
    mg = mg_ref[...]
    parts = []
    for hd in range(M_HEADS):
        sl = slice(hd * M_HEAD_DIM, (hd + 1) * M_HEAD_DIM)
        parts.append(_rms(h[:, sl], mg[:, sl]).astype(BF16))
    m_out = jnp.concatenate(parts, axis=1)
    merged = (sga_ref[...].astype(F32) * _dot(a_ref[...], wpa_ref[...])
              + sgb_ref[...].astype(F32) * _dot(m_out, wpb_ref[...]))
    o_ref[...] = x_ref[...] + _dot(merged.astype(BF16), wo_ref[...])


def _merge(x2, a2, hf2, hb2, mo2, sga2, sgb2, wts, tb):
    n, d = x2.shape
    row = lambda w: pl.BlockSpec((tb, w), lambda i: (i, 0))
    return pl.pallas_call(
        _merge_kernel,
        grid=(n // tb,),
        in_specs=[row(d), row(a2.shape[1]), row(M_WIDTH), row(M_WIDTH), row(M_WIDTH), row(d), row(d)]
        + [_const_spec(w.shape) for w in wts],
        out_specs=row(d),
        out_shape=jax.ShapeDtypeStruct((n, d), F32),
        compiler_params=pltpu.CompilerParams(dimension_semantics=("parallel",), vmem_limit_bytes=VMEM_LIMIT),
        name="merge",
    )(x2, a2, hf2, hb2, mo2, sga2, sgb2, *wts)


def _ffn_kernel(x_ref, n2_ref, wg_ref, wu_ref, wd_ref, fn_ref, o_ref):
    x = x_ref[...]
    xn = _rms(x, n2_ref[...]).astype(BF16)
    g = _dot(xn, wg_ref[...])
    u = _dot(xn, wu_ref[...])
    act = (g * _sigmoid(g) * u).astype(BF16)
    y = x + _dot(act, wd_ref[...])
    o_ref[...] = _rms(y, fn_ref[...])


def _ffn(x2, wts, tb):
    n, d = x2.shape
    row = pl.BlockSpec((tb, d), lambda i: (i, 0))
    return pl.pallas_call(
        _ffn_kernel,
        grid=(n // tb,),
        in_specs=[row] + [_const_spec(w.shape) for w in wts],
        out_specs=row,
        out_shape=jax.ShapeDtypeStruct((n, d), F32),
        compiler_params=pltpu.CompilerParams(dimension_semantics=("parallel",), vmem_limit_bytes=VMEM_LIMIT),
        name="ffn",
    )(x2, *wts)


def _rope_tables(pos):
    half = QK_ROPE // 2
    freqs = ROPE_THETA ** (-jnp.arange(half, dtype=F32) / half)
    ang = pos.astype(F32)[:, None] * freqs[None, :]
    c, s = jnp.cos(ang), jnp.sin(ang)
    n = pos.shape[0]
    pad = jnp.zeros((n, LANES - QK_DIM), F32)
    cos = jnp.concatenate([jnp.ones((n, QK_NOPE), F32), c, c, pad], axis=1)
    sin = jnp.concatenate([jnp.zeros((n, QK_NOPE), F32), s, s, pad], axis=1)
    return cos, sin


def _rot(w):
    half = QK_ROPE // 2
    return jnp.concatenate([-w[..., half:], w[..., :half]], axis=-1)


def _pad_cols(w, left, total):
    return jnp.pad(w, [(0, 0)] * (w.ndim - 1) + [(left, total - left - w.shape[-1])])


def _prep_weights(norm1_g, w_in, b_in, q_norm_g, kv_norm_g, w_uq, w_ukv):
    d = w_in.shape[0]
    o_kr = Q_LORA + KV_LORA
    o_m = o_kr + QK_ROPE
    o_g = o_m + 4 * M_WIDTH
    o_ga = o_g + 4 * M_HEADS

    def seg1(w):
        kr = w[..., o_kr:o_m]
        return jnp.concatenate([w[..., :o_kr], _pad_cols(kr, QK_NOPE, LANES), _pad_cols(_rot(kr), QK_NOPE, LANES)], axis=-1)

    def segm(w):
        gates = w[..., o_g:o_ga].reshape(w.shape[:-1] + (4, M_HEADS))
        g_in = jnp.concatenate([gates[..., 0, :], gates[..., 2, :]], axis=-1)
        g_fg = jnp.concatenate([gates[..., 1, :], gates[..., 3, :]], axis=-1)
        return jnp.concatenate([w[..., o_m:o_m + 2 * M_WIDTH], w[..., o_m + 3 * M_WIDTH:o_g],
                                _pad_cols(g_in, 0, LANES), _pad_cols(g_fg, 0, LANES)], axis=-1)

    b2 = b_in[None, :]
    w1, b1 = seg1(w_in).astype(BF16), seg1(b2)
    wm, bm = segm(w_in).astype(BF16), segm(b2)
    wg, bg = w_in[:, o_ga:].astype(BF16), b2[:, o_ga:]

    uq = w_uq.reshape(Q_LORA, MLA_HEADS, QK_DIM)
    wuq = _pad_cols(uq, 0, LANES).reshape(Q_LORA, MLA_HEADS * LANES).astype(BF16)
    wuqr = _pad_cols(_rot(uq[..., QK_NOPE:]), QK_NOPE, LANES).reshape(Q_LORA, MLA_HEADS * LANES).astype(BF16)
    ukv = w_ukv.reshape(KV_LORA, MLA_HEADS, QK_NOPE + V_HEAD)
    wuk = _pad_cols(ukv[..., :QK_NOPE], 0, LANES).reshape(KV_LORA, MLA_HEADS * LANES).astype(BF16)
    uv = ukv[..., QK_NOPE:].reshape(KV_LORA, MLA_HEADS // 2, 2, V_HEAD)
    wuv = jnp.stack([_pad_cols(uv[:, :, 0], 0, LANES), _pad_cols(uv[:, :, 1], V_HEAD, LANES)], axis=2)
    wuv = wuv.reshape(KV_LORA, MLA_HEADS * LANES).T.astype(BF16)
    lane = jnp.arange(2 * LANES)
    vone = jnp.tile(((lane == V_HEAD) | (lane == LANES)).astype(F32), MLA_HEADS // 2)[:, None]
    o_v = o_m + 2 * M_WIDTH
    wmv, bmv = w_in[:, o_v:o_v + M_WIDTH].T.astype(BF16), b_in[o_v:o_v + M_WIDTH][:, None]
    return (norm1_g[None, :], w1, b1, wm, bm, wg, bg, q_norm_g[None, :], kv_norm_g[None, :], wuq, wuqr, wuk, wuv, vone,
            wmv, bmv)


def _pick(n, pref):
    return pref if n % pref == 0 else n


def _trunk(x, rope, meta_parts, in_wts, conv_w, conv_b, merge_wts, ffn_wts):
    b, s, d = x.shape
    n = b * s
    tb = _pick(s, 512)
    k_m, vt_m, mqk_m, mvt_m, gt_m = meta_parts

    x2 = x.reshape(n, d)
    q, k, vt, mqk, mvt, mo, gt, sga, sgb = _in_proj(x2, *rope, in_wts, tb, s)

    r3 = lambda a: a.reshape(b, s, a.shape[-1])
    bias = jnp.where(jnp.arange(MCHUNK) >= META_PAD, 0.0, NEG).astype(F32)[:, None]
    a_out = _attention(r3(q), k.reshape(n // tb, tb, k.shape[-1]), vt, k_m, vt_m, bias, _pick(s, 512))

    gate_rows = lambda g: jnp.swapaxes(
        jnp.concatenate([g[..., :2 * M_HEADS], g[..., LANES:LANES + 2 * M_HEADS]], axis=-1), -1, -2)
    gt3 = r3(gt)
    h_f, h_b = _mlstm(r3(mqk), mvt, gt3, gate_rows(gt3), mqk_m, mvt_m, gate_rows(gt_m), conv_w, conv_b)

    h1 = _merge(x2, a_out.reshape(n, -1), h_f.reshape(n, -1), h_b.reshape(n, -1), mo, sga, sgb, merge_wts, tb)
    return _ffn(h1, ffn_wts, tb).reshape(b, s, d)


def kernel(x_prompt, x_sample, meta_tokens, norm1_g, w_in, b_in, conv_w, conv_b, q_norm_g, kv_norm_g, w_uq, w_ukv, m_norm_g, w_pa, w_pb, w_o, norm2_g, w_ffn_gate, w_ffn_up, w_ffn_down, final_norm_g):
    assert w_in.shape[0] == 1, "single-layer trunk"
    d = x_prompt.shape[-1]
    in_wts = _prep_weights(norm1_g[0], w_in[0], b_in[0], q_norm_g[0], kv_norm_g[0], w_uq[0], w_ukv[0])
    merge_wts = (m_norm_g[0][None, :], w_pa[0].astype(BF16), w_pb[0].astype(BF16), w_o[0].astype(BF16))
    ffn_wts = (norm2_g[0][None, :], w_ffn_gate[0].astype(BF16), w_ffn_up[0].astype(BF16),
               w_ffn_down[0].astype(BF16), final_norm_g[None, :])
    cw, cb = conv_w[0], conv_b[0][None, :]

    hm = jnp.concatenate([jnp.zeros((META_PAD, d), x_prompt.dtype), meta_tokens.astype(x_prompt.dtype)], axis=0)
    cos_m, sin_m = _rope_tables(jnp.arange(MCHUNK) - META_PAD)
    _, k_m, vt_m, mqk_m, mvt_m, _, gt_m, _, _ = _in_proj(hm, cos_m, sin_m, in_wts, MCHUNK, MCHUNK)
    meta_parts = (k_m, vt_m[0], mqk_m, mvt_m[0], gt_m)

    rope = _rope_tables(N_META + jnp.arange(max(x_prompt.shape[1], x_sample.shape[1])))
    y_prompt = _trunk(x_prompt, rope, meta_parts, in_wts, cw, cb, merge_wts, ffn_wts)
    y_sample = _trunk(x_sample, rope, meta_parts, in_wts, cw, cb, merge_wts, ffn_wts)
    return (y_prompt, y_sample)
```

```python
import functools

import jax
import jax.numpy as jnp
from jax import lax
from jax.experimental import pallas as pl
from jax.experimental.pallas import tpu as pltpu

F32 = jnp.float32
BF16 = jnp.bfloat16

EPS = 1e-6
N_META = 16
ROPE_THETA = 10000.0
MLA_HEADS = 8
QK_NOPE = 64
QK_ROPE = 32
V_HEAD = 64
QK_DIM = QK_NOPE + QK_ROPE
Q_LORA = 256
KV_LORA = 128
M_HEADS = 4
M_HEAD_DIM = 128
M_WIDTH = M_HEADS * M_HEAD_DIM
LANES = 128
MCHUNK = 128
LCHUNK = 256
META_PAD = MCHUNK - N_META
HALO = 16
NEG = -1e30
LOG2E = 1.4426950408889634
GROWTH_LIMIT = 64.0
FAST_UNROLL = 16
VMEM_LIMIT = 56 * 1024 * 1024


def _dot(a, b):
    return jnp.dot(a, b, preferred_element_type=F32)


def _dot_nt(a, b):
    return lax.dot_general(a, b, (((1,), (1,)), ((), ())), preferred_element_type=F32)


def _dot_tn(a, b):
    return lax.dot_general(a, b, (((0,), (0,)), ((), ())), preferred_element_type=F32)


def _sigmoid(x):
    return 1.0 / (1.0 + jnp.exp(-x))


def _rms(x, g):
    return x * lax.rsqrt(jnp.mean(x * x, axis=-1, keepdims=True) + EPS) * g


def _const_spec(shape):
    return pl.BlockSpec(shape, lambda *_: (0,) * len(shape))


def _in_proj_kernel(x_ref, cos_ref, sin_ref, n1_ref, w1_ref, b1_ref, wm_ref, bm_ref, wg_ref, bg_ref,
                    qn_ref, kvn_ref, wuq_ref, wuqr_ref, wuk_ref, wuv_ref, vone_ref, wmv_ref, bmv_ref,
                    q_out, k_out, v_out, mqk_out, mv_out, mo_out, gt_out, sga_out, sgb_out):
    xn = _rms(x_ref[...], n1_ref[...]).astype(BF16)
    cos = cos_ref[...]
    sin = sin_ref[...]

    p1 = _dot(xn, w1_ref[...]) + b1_ref[...]
    c_q = p1[:, :Q_LORA]
    c_kv = p1[:, Q_LORA:Q_LORA + KV_LORA]
    kr_plain = p1[:, Q_LORA + KV_LORA:Q_LORA + KV_LORA + LANES]
    kr_rot = p1[:, Q_LORA + KV_LORA + LANES:]
    k_rope = kr_plain * cos + kr_rot * sin

    cqn = (_rms(c_q, qn_ref[...]) * (QK_DIM ** -0.5 * LOG2E)).astype(BF16)
    qa = _dot(cqn, wuq_ref[...])
    qb = _dot(cqn, wuqr_ref[...])
    ckvn = _rms(c_kv, kvn_ref[...]).astype(BF16)
    ka = _dot(ckvn, wuk_ref[...])
    for h in range(MLA_HEADS):
        sl = slice(h * LANES, (h + 1) * LANES)
        q_out[:, sl] = (qa[:, sl] * cos + qb[:, sl] * sin).astype(BF16)
        k_out[:, sl] = (ka[:, sl] + k_rope).astype(BF16)
    v_out[...] = (_dot_nt(wuv_ref[...], ckvn) + vone_ref[...]).astype(BF16)

    pm = _dot(xn, wm_ref[...]) + bm_ref[...]
    mqk_out[...] = pm[:, :2 * M_WIDTH].astype(BF16)
    mo_out[...] = pm[:, 2 * M_WIDTH:3 * M_WIDTH].astype(BF16)
    gt_out[...] = pm[:, 3 * M_WIDTH:]
    mv_out[...] = (_dot_nt(wmv_ref[...], xn) + bmv_ref[...]).astype(BF16)

    pg = _dot(xn, wg_ref[...]) + bg_ref[...]
    d = sga_out.shape[-1]
    sga_out[...] = _sigmoid(pg[:, :d]).astype(BF16)
    sgb_out[...] = _sigmoid(pg[:, d:]).astype(BF16)


def _in_proj(x2, cos, sin, wts, tb, seq):
    n, d = x2.shape
    rope = pl.BlockSpec((tb, LANES), lambda i: (i % (seq // tb), 0))
    widths = (MLA_HEADS * LANES, MLA_HEADS * LANES, MLA_HEADS * VROWS, 2 * M_WIDTH, M_WIDTH, M_WIDTH, 2 * LANES, d, d)
    dtypes = (BF16, BF16, BF16, BF16, BF16, BF16, F32, BF16, BF16)
    row = lambda w: pl.BlockSpec((tb, w), lambda i: (i, 0))
    out_specs = [row(w) for w in widths]
    out_shape = [jax.ShapeDtypeStruct((n, w), t) for w, t in zip(widths, dtypes)]
    for o in (2, 4):
        out_specs[o] = pl.BlockSpec((None, widths[o], tb), lambda i: (i, 0, 0))
        out_shape[o] = jax.ShapeDtypeStruct((n // tb, widths[o], tb), BF16)
    return pl.pallas_call(
        _in_proj_kernel,
        grid=(n // tb,),
        in_specs=[row(d), rope, rope] + [_const_spec(w.shape) for w in wts],
        out_specs=out_specs,
        out_shape=out_shape,
        compiler_params=pltpu.CompilerParams(dimension_semantics=("parallel",), vmem_limit_bytes=VMEM_LIMIT),
        name="in_proj",
    )(x2, cos, sin, *wts)


def _attn_kernel(q_ref, k_ref, vt_ref, km_ref, vtm_ref, bias_ref, o_ref,
                 m_scr, acc_scr, g_scr, mx_scr, al_scr, s_scr, p_scr, *, nk):
    hsl = [slice(0, LANES), slice(LANES, 2 * LANES)]
    vsl = [slice(0, VROWS), slice(VROWS, 2 * VROWS)]
    qs = [q_ref[:, sl] for sl in hsl]

    def meta_init():
        for hh, sl in enumerate(hsl):
            s = _dot_nt(km_ref[:, sl], qs[hh]) + bias_ref[...]
            m = jnp.max(s, axis=0, keepdims=True)
            p = jnp.exp2(s - m).astype(BF16)
            m_scr[hh] = m
            acc_scr[hh] = _dot(vtm_ref[vsl[hh], :], p)

    def pipeline(first, step, n_steps, last, unroll):
        first()

        def body(t, carry):
            for u in range(unroll):
                step(unroll * t + u, u % 2)
            return carry

        n_loop = n_steps // unroll
        lax.fori_loop(0, n_loop, body, 0)
        for i in range(n_loop * unroll, n_steps):
            step(i, i % 2)
        last()

    def scores_exp(c, slot):
        for hh, sl in enumerate(hsl):
            s = _dot_nt(k_ref[c, :, sl], qs[hh])
            ref = m_scr[hh]
            p_scr[slot, hh] = jnp.exp2(s - ref).astype(BF16)
            m_new = jnp.maximum(ref, jnp.max(s, axis=0, keepdims=True))
            al_scr[slot, hh] = jnp.exp2(ref - m_new)
            g_scr[hh] = jnp.maximum(g_scr[hh], m_new - ref)
            m_scr[hh] = m_new

    def values_rebase(c, slot):
        for hh, sl in enumerate(hsl):
            acc_scr[hh] = (acc_scr[hh] + _dot(vt_ref[c, vsl[hh], :], p_scr[slot, hh])) * al_scr[slot, hh]

    def fast_step(i, par):
        scores_exp(i + 1, 1 - par)
        values_rebase(i, par)

    meta_init()
    g_scr[...] = jnp.zeros_like(g_scr)
    pipeline(lambda: scores_exp(0, 0), fast_step, nk - 1, lambda: values_rebase(nk - 1, (nk - 1) % 2), FAST_UNROLL)

    def scores(c, slot):
        for hh, sl in enumerate(hsl):
            s = _dot_nt(k_ref[c, :, sl], qs[hh])
            mx_scr[slot, hh] = jnp.max(s, axis=0, keepdims=True)
            s_scr[slot, hh] = s

    def softmax(slot):
        for hh in range(2):
            m_prev = m_scr[hh]
            m_new = jnp.maximum(m_prev, mx_scr[slot, hh])
            al_scr[slot, hh] = jnp.exp2(m_prev - m_new)
            m_scr[hh] = m_new
            p_scr[slot, hh] = jnp.exp2(s_scr[slot, hh] - m_new).astype(BF16)

    def values(c, slot):
        for hh, sl in enumerate(hsl):
            acc_scr[hh] = al_scr[slot, hh] * acc_scr[hh] + _dot(vt_ref[c, vsl[hh], :], p_scr[slot, hh])

    def safe_first():
        scores(0, 0)
        scores(1, 1)
        softmax(0)

    def safe_step(i, par):
        scores(i + 2, par)
        softmax(1 - par)
        values(i, par)

    def safe_last():
        softmax(1)
        values(nk - 2, 0)
        values(nk - 1, 1)

    @pl.when(jnp.max(jnp.maximum(g_scr[0], g_scr[1])) > GROWTH_LIMIT)
    def _():
        meta_init()
        pipeline(safe_first, safe_step, nk - 2, safe_last, 2)

    o_t = [acc_scr[hh][:V_HEAD, :] / acc_scr[hh][V_HEAD:V_HEAD + 1, :] for hh in range(2)]
    o_ref[...] = jnp.concatenate(o_t, axis=0).T.astype(BF16)


def _attention(q, k, vt, km, vtm, bias, tq):
    b, s, _ = q.shape
    nk = k.shape[0] // b
    assert nk >= 2 and nk % 2 == 0, "the key-chunk pipeline is unrolled by two"
    tk = k.shape[1]
    hp = MLA_HEADS // 2
    w2 = 2 * LANES
    return pl.pallas_call(
        functools.partial(_attn_kernel, nk=nk),
        grid=(b, hp, s // tq),
        in_specs=[
            pl.BlockSpec((None, tq, w2), lambda bi, h, i: (bi, i, h)),
            pl.BlockSpec((nk, tk, w2), lambda bi, h, i: (bi, 0, h)),
            pl.BlockSpec((nk, 2 * VROWS, tk), lambda bi, h, i: (bi, h, 0)),
            pl.BlockSpec((MCHUNK, w2), lambda bi, h, i: (0, h)),
            pl.BlockSpec((2 * VROWS, MCHUNK), lambda bi, h, i: (h, 0)),
            pl.BlockSpec((MCHUNK, 1), lambda bi, h, i: (0, 0)),
        ],
        out_specs=pl.BlockSpec((None, tq, LANES), lambda bi, h, i: (bi, i, h)),
        out_shape=jax.ShapeDtypeStruct((b, s, hp * LANES), BF16),
        scratch_shapes=[
            pltpu.VMEM((2, 1, tq), F32), pltpu.VMEM((2, VROWS, tq), F32),
            pltpu.VMEM((2, 1, tq), F32),
            pltpu.VMEM((2, 2, 1, tq), F32), pltpu.VMEM((2, 2, 1, tq), F32),
            pltpu.VMEM((2, 2, tk, tq), F32), pltpu.VMEM((2, 2, tk, tq), BF16),
        ],
        compiler_params=pltpu.CompilerParams(
            dimension_semantics=("parallel", "parallel", "arbitrary"), vmem_limit_bytes=VMEM_LIMIT),
        name="mla_attention",
    )(q, k, vt, km, vtm, bias)


def _conv_silu(cur, prev_row, next_row, cw, cb):
    n = cur.shape[0]
    ridx = lax.broadcasted_iota(jnp.int32, cur.shape, 0)
    x_prev = jnp.where(ridx == 0, prev_row, pltpu.roll(cur, 1, axis=0))
    x_next = jnp.where(ridx == n - 1, next_row, pltpu.roll(cur, n - 1, axis=0))
    y = cw[0:1, :] * x_prev + cw[1:2, :] * cur + cw[2:3, :] * x_next + cb
    return y * _sigmoid(y)


def _log_sigmoid(x):
    return jnp.minimum(x, 0.0) - jnp.log(1.0 + jnp.exp(-jnp.abs(x)))


def _tri(n, upper):
    r = lax.broadcasted_iota(jnp.int32, (n, n), 0)
    c = lax.broadcasted_iota(jnp.int32, (n, n), 1)
    return (c >= r) if upper else (c <= r)


FF_CHUNK = 1024
VROWS = V_HEAD + 16
AUG = 16
CROWS = M_HEAD_DIM + AUG


def _v_aug_t(vt_h):
    r = lax.broadcasted_iota(jnp.int32, (AUG, vt_h.shape[1]), 0)
    return jnp.concatenate([vt_h, jnp.where(r == 0, 1.0, 0.0).astype(vt_h.dtype)], axis=0)


def _gate_rows(gi8, gf8, valid, tri_t, last, m_prev8):
    a8 = _log_sigmoid(gf8)
    b8 = gi8
    if valid is not None:
        a8 = jnp.where(valid, a8, 0.0)
        b8 = jnp.where(valid, b8, NEG)
    f8 = jnp.dot(a8, tri_t, precision=lax.Precision.HIGHEST, preferred_element_type=F32)
    f_tot = f8[:, last:last + 1]
    g8 = f_tot - f8 + b8
    m_new8 = jnp.maximum(f_tot + m_prev8, jnp.max(g8, axis=1, keepdims=True))
    return f8, jnp.exp(g8 - m_new8), jnp.exp(f_tot + m_prev8 - m_new8), m_new8


def _state_update(k_h, vaug_t, w_row, decay, ct_ref, h):
    vw = (vaug_t.astype(F32) * w_row).astype(BF16)
    ct_ref[h] = decay * ct_ref[h] + _dot(vw, k_h)


def _mlstm_chunk(q, k, vt_ref, gc, gr, ct_ref, m_ref, reverse):
    n = q.shape[0]
    r0 = M_HEADS if reverse else 0
    mask_t = _tri(n, not reverse)
    tri_t = mask_t.astype(F32)
    tri_c = _tri(n, reverse).astype(F32)
    last = 0 if reverse else n - 1
    m_prev8 = m_ref[:, 0:1]
    f8, w8, decay8, m_new8 = _gate_rows(gr[0:8, :], gr[8:16, :], None, tri_t, last, m_prev8)
    inter8 = f8 + m_prev8
    c_cols = jnp.dot(tri_c, _log_sigmoid(gc[:, LANES:]), precision=lax.Precision.HIGHEST,
                     preferred_element_type=F32) - gc[:, :LANES]
    outs = []
    for h in range(M_HEADS):
        r = r0 + h
        sl = slice(h * M_HEAD_DIM, (h + 1) * M_HEAD_DIM)
        q_h = q[:, sl].astype(BF16)
        k_h = k[:, sl].astype(BF16)
        vaug_t = _v_aug_t(vt_ref[sl, :])
        inter = inter8[r:r + 1, :]
        log_d = jnp.where(mask_t, f8[r:r + 1, :] - c_cols[:, r:r + 1], NEG)
        m_j = jnp.maximum(inter, jnp.max(log_d, axis=0, keepdims=True))
        s_t = (_dot_nt(k_h, q_h) * jnp.exp(log_d - m_j)).astype(BF16)
        h_t = _dot_nt(ct_ref[h].astype(BF16), q_h) * jnp.exp(inter - m_j) + _dot(vaug_t, s_t)
        den = h_t[M_HEAD_DIM:M_HEAD_DIM + 1, :]
        out_t = h_t[:M_HEAD_DIM, :] / jnp.maximum(jnp.abs(den), jnp.exp(-m_j))
        outs.append(out_t.T)
        _state_update(k_h, vaug_t, w8[r:r + 1, :], decay8[r:r + 1, :], ct_ref, h)
    m_ref[...] = jnp.broadcast_to(m_new8, m_ref.shape)
    return jnp.concatenate(outs, axis=1)


def _mlstm_kernel(qk_f, pv_f, nx_f, vt_f, gc_f, gr_f, qk_b, pv_b, nx_b, vt_b, gc_b, gr_b,
                  qk_m, vt_m, gr_m, cw_ref, cb_ref, hf_out, hb_out, cf_scr, mf_scr, cb_scr, mb_scr):
    j = pl.program_id(1)
    nc = pl.num_programs(1)
    cw = cw_ref[...]
    cb = cb_ref[...]
    scale = M_HEAD_DIM ** -0.5
    meta_last = qk_m[MCHUNK - 1:MCHUNK, :].astype(F32)

    @pl.when(j == 0)
    def _():
        cf_scr[...] = jnp.zeros_like(cf_scr)
        cb_scr[...] = jnp.zeros_like(cb_scr)
        mb_scr[...] = jnp.zeros_like(mb_scr)
        valid_col = lax.broadcasted_iota(jnp.int32, (MCHUNK, 1), 0) >= META_PAD
        valid_row = lax.broadcasted_iota(jnp.int32, (1, MCHUNK), 1) >= META_PAD
        cur = jnp.where(valid_col, qk_m[...].astype(F32), 0.0)
        first_x = qk_f[0:1, :].astype(F32)
        k_all = _conv_silu(cur, jnp.zeros_like(first_x), first_x, cw, cb)[:, M_WIDTH:] * scale
        _, w8, decay8, m_new8 = _gate_rows(gr_m[0:8, :], gr_m[8:16, :], valid_row, _tri(MCHUNK, True).astype(F32),
                                           MCHUNK - 1, jnp.zeros((8, 1), F32))
        for h in range(M_HEADS):
            sl = slice(h * M_HEAD_DIM, (h + 1) * M_HEAD_DIM)
            _state_update(k_all[:, sl].astype(BF16), _v_aug_t(vt_m[sl, :]), w8[h:h + 1, :], decay8[h:h + 1, :],
                          cf_scr, h)
        mf_scr[...] = jnp.broadcast_to(m_new8, mf_scr.shape)

    def run(qk_ref, pv_ref, nx_ref, vt_ref, gc_ref, gr_ref, c, c_scr, m_scr, reverse, out_ref):
        prev_row = jnp.where(c == 0, meta_last, pv_ref[HALO - 1:HALO, :].astype(F32))
        next_row = jnp.where(c == nc - 1, 0.0, nx_ref[0:1, :].astype(F32))
        qk = _conv_silu(qk_ref[...].astype(F32), prev_row, next_row, cw, cb)
        h = _mlstm_chunk(qk[:, :M_WIDTH], qk[:, M_WIDTH:] * scale, vt_ref, gc_ref[...], gr_ref[...],
                         c_scr, m_scr, reverse)
        out_ref[...] = h.astype(out_ref.dtype)

    run(qk_f, pv_f, nx_f, vt_f, gc_f, gr_f, j, cf_scr, mf_scr, False, hf_out)
    run(qk_b, pv_b, nx_b, vt_b, gc_b, gr_b, nc - 1 - j, cb_scr, mb_scr, True, hb_out)


def _mlstm(mqk, mvt, gcol, grow, mqk_m, mvt_m, grow_m, conv_w, conv_b):
    b, s, _ = mqk.shape
    nc = s // LCHUNK
    hpc = LCHUNK // HALO
    nhb = s // HALO
    cps = mvt.shape[2] // LCHUNK
    spb = mvt.shape[0] // b

    def specs(chunk_of):
        return [
            pl.BlockSpec((None, LCHUNK, 2 * M_WIDTH), lambda bi, j: (bi, chunk_of(j), 0)),
            pl.BlockSpec((None, HALO, 2 * M_WIDTH), lambda bi, j: (bi, jnp.maximum(chunk_of(j) * hpc - 1, 0), 0)),
            pl.BlockSpec((None, HALO, 2 * M_WIDTH),
                         lambda bi, j: (bi, jnp.minimum((chunk_of(j) + 1) * hpc, nhb - 1), 0)),
            pl.BlockSpec((None, M_WIDTH, LCHUNK), lambda bi, j: (bi * spb + chunk_of(j) // cps, 0, chunk_of(j) % cps)),
            pl.BlockSpec((None, LCHUNK, 2 * LANES), lambda bi, j: (bi, chunk_of(j), 0)),
            pl.BlockSpec((None, 4 * M_HEADS, LCHUNK), lambda bi, j: (bi, 0, chunk_of(j))),
        ]

    fwd = lambda j: j
    bwd = lambda j: nc - 1 - j
    operands = [mqk, mqk, mqk, mvt, gcol, grow]
    out_spec = lambda chunk_of: pl.BlockSpec((None, LCHUNK, M_WIDTH), lambda bi, j: (bi, chunk_of(j), 0))
    state = [pltpu.VMEM((M_HEADS, CROWS, M_HEAD_DIM), F32), pltpu.VMEM((2 * M_HEADS, LANES), F32)]
    return pl.pallas_call(
        _mlstm_kernel,
        grid=(b, nc),
        in_specs=specs(fwd) + specs(bwd) + [_const_spec(a.shape) for a in (mqk_m, mvt_m, grow_m, conv_w, conv_b)],
        out_specs=[out_spec(fwd), out_spec(bwd)],
        out_shape=[jax.ShapeDtypeStruct((b, s, M_WIDTH), BF16)] * 2,
        scratch_shapes=state + state,
        compiler_params=pltpu.CompilerParams(
            dimension_semantics=("parallel", "arbitrary"), vmem_limit_bytes=VMEM_LIMIT),
        name="mlstm",
    )(*operands, *operands, mqk_m, mvt_m, grow_m, conv_w, conv_b)


def _out_kernel(x_ref, a_ref, hf_ref, hb_ref, mo_ref, sga_ref, sgb_ref,
                mg_ref, wpa_ref, wpb_ref, wo_ref, n2_ref, wg_ref, wu_ref, wd_ref, fn_ref, o_ref):
    h = (hf_ref[...].astype(F32) + hb_ref[...].astype(F32)) * _sigmoid(mo_ref[...].astype(F32))
    mg = mg_ref[...]
    parts = []
    for hd in range(M_HEADS):
        sl = slice(hd * M_HEAD_DIM, (hd + 1) * M_HEAD_DIM)
        parts.append(_rms(h[:, sl], mg[:, sl]).astype(BF16))
    m_out = jnp.concatenate(parts, axis=1)
    merged = (sga_ref[...].astype(F32) * _dot(a_ref[...], wpa_ref[...])
              + sgb_ref[...].astype(F32) * _dot(m_out, wpb_ref[...]))
    y = x_ref[...] + _dot(merged.astype(BF16), wo_ref[...])

    xn = _rms(y, n2_ref[...]).astype(BF16)
    d_ff = wg_ref.shape[1]
    for c0 in range(0, d_ff, FF_CHUNK):
        c1 = min(c0 + FF_CHUNK, d_ff)
        g = _dot(xn, wg_ref[:, c0:c1])
        u = _dot(xn, wu_ref[:, c0:c1])
        y = y + _dot((g * _sigmoid(g) * u).astype(BF16), wd_ref[c0:c1, :])
    o_ref[...] = _rms(y, fn_ref[...])


def _out_proj(x2, a2, hf2, hb2, mo2, sga2, sgb2, wts, tb):
    n, d = x2.shape
    row = lambda w: pl.BlockSpec((tb, w), lambda i: (i, 0))
    const = lambda w: pl.BlockSpec(w.shape, lambda i: (0,) * w.ndim, pipeline_mode=pl.Buffered(1))
    return pl.pallas_call(
        _out_kernel,
        grid=(n // tb,),
        in_specs=[row(d), row(a2.shape[1]), row(M_WIDTH), row(M_WIDTH), row(M_WIDTH), row(d), row(d)]
        + [const(w) for w in wts],
        out_specs=row(d),
        out_shape=jax.ShapeDtypeStruct((n, d), F32),
        compiler_params=pltpu.CompilerParams(dimension_semantics=("parallel",), vmem_limit_bytes=VMEM_LIMIT),
        name="out_proj",
    )(x2, a2, hf2, hb2, mo2, sga2, sgb2, *wts)


def _rope_tables(pos):
    half = QK_ROPE // 2
    freqs = ROPE_THETA ** (-jnp.arange(half, dtype=F32) / half)
    ang = pos.astype(F32)[:, None] * freqs[None, :]
    c, s = jnp.cos(ang), jnp.sin(ang)
    n = pos.shape[0]
    pad = jnp.zeros((n, LANES - QK_DIM), F32)
    cos = jnp.concatenate([jnp.ones((n, QK_NOPE), F32), c, c, pad], axis=1)
    sin = jnp.concatenate([jnp.zeros((n, QK_NOPE), F32), s, s, pad], axis=1)
    return cos, sin


def _rot(w):
    half = QK_ROPE // 2
    return jnp.concatenate([-w[..., half:], w[..., :half]], axis=-1)


def _pad_cols(w, left, total):
    return jnp.pad(w, [(0, 0)] * (w.ndim - 1) + [(left, total - left - w.shape[-1])])


def _prep_weights(norm1_g, w_in, b_in, q_norm_g, kv_norm_g, w_uq, w_ukv):
    d = w_in.shape[0]
    o_kr = Q_LORA + KV_LORA
    o_m = o_kr + QK_ROPE
    o_g = o_m + 4 * M_WIDTH
    o_ga = o_g + 4 * M_HEADS

    def seg1(w):
        kr = w[..., o_kr:o_m]
        return jnp.concatenate([w[..., :o_kr], _pad_cols(kr, QK_NOPE, LANES), _pad_cols(_rot(kr), QK_NOPE, LANES)], axis=-1)

    def segm(w):
        gates = w[..., o_g:o_ga].reshape(w.shape[:-1] + (4, M_HEADS))
        g_in = jnp.concatenate([gates[..., 0, :], gates[..., 2, :]], axis=-1)
        g_fg = jnp.concatenate([gates[..., 1, :], gates[..., 3, :]], axis=-1)
        return jnp.concatenate([w[..., o_m:o_m + 2 * M_WIDTH], w[..., o_m + 3 * M_WIDTH:o_g],
                                _pad_cols(g_in, 0, LANES), _pad_cols(g_fg, 0, LANES)], axis=-1)

    b2 = b_in[None, :]
    w1, b1 = seg1(w_in).astype(BF16), seg1(b2)
    wm, bm = segm(w_in).astype(BF16), segm(b2)
    wg, bg = w_in[:, o_ga:].astype(BF16), b2[:, o_ga:]

    uq = w_uq.reshape(Q_LORA, MLA_HEADS, QK_DIM)
    wuq = _pad_cols(uq, 0, LANES).reshape(Q_LORA, MLA_HEADS * LANES).astype(BF16)
    wuqr = _pad_cols(_rot(uq[..., QK_NOPE:]), QK_NOPE, LANES).reshape(Q_LORA, MLA_HEADS * LANES).astype(BF16)
    ukv = w_ukv.reshape(KV_LORA, MLA_HEADS, QK_NOPE + V_HEAD)
    wuk = _pad_cols(ukv[..., :QK_NOPE], 0, LANES).reshape(KV_LORA, MLA_HEADS * LANES).astype(BF16)
    wuv = _pad_cols(ukv[..., QK_NOPE:], 0, VROWS).reshape(KV_LORA, MLA_HEADS * VROWS).T.astype(BF16)
    vone = jnp.tile((jnp.arange(VROWS) == V_HEAD).astype(F32), MLA_HEADS)[:, None]
    o_v = o_m + 2 * M_WIDTH
    wmv, bmv = w_in[:, o_v:o_v + M_WIDTH].T.astype(BF16), b_in[o_v:o_v + M_WIDTH][:, None]
    return (norm1_g[None, :], w1, b1, wm, bm, wg, bg, q_norm_g[None, :], kv_norm_g[None, :], wuq, wuqr, wuk, wuv, vone,
            wmv, bmv)


def _pick(n, pref):
    return pref if n % pref == 0 else n


def _trunk(x, rope, meta_parts, in_wts, conv_w, conv_b, merge_wts, ffn_wts):
    b, s, d = x.shape
    n = b * s
    tb = _pick(s, 512)
    k_m, vt_m, mqk_m, mvt_m, gt_m = meta_parts

    x2 = x.reshape(n, d)
    q, k, vt, mqk, mvt, mo, gt, sga, sgb = _in_proj(x2, *rope, in_wts, tb, s)

    r3 = lambda a: a.reshape(b, s, a.shape[-1])
    bias = jnp.where(jnp.arange(MCHUNK) >= META_PAD, 0.0, NEG).astype(F32)[:, None]
    a_out = _attention(r3(q), k.reshape(n // tb, tb, k.shape[-1]), vt, k_m, vt_m, bias, _pick(s, 512))

    gate_rows = lambda g: jnp.swapaxes(
        jnp.concatenate([g[..., :2 * M_HEADS], g[..., LANES:LANES + 2 * M_HEADS]], axis=-1), -1, -2)
    gt3 = r3(gt)
    h_f, h_b = _mlstm(r3(mqk), mvt, gt3, gate_rows(gt3), mqk_m, mvt_m, gate_rows(gt_m), conv_w, conv_b)

    y = _out_proj(x2, a_out.reshape(n, -1), h_f.reshape(n, -1), h_b.reshape(n, -1), mo, sga, sgb,
                  merge_wts + ffn_wts, tb)
    return y.reshape(b, s, d)


def kernel(x_prompt, x_sample, meta_tokens, norm1_g, w_in, b_in, conv_w, conv_b, q_norm_g, kv_norm_g, w_uq, w_ukv, m_norm_g, w_pa, w_pb, w_o, norm2_g, w_ffn_gate, w_ffn_up, w_ffn_down, final_norm_g):
    assert w_in.shape[0] == 1, "single-layer trunk"
    d = x_prompt.shape[-1]
    in_wts = _prep_weights(norm1_g[0], w_in[0], b_in[0], q_norm_g[0], kv_norm_g[0], w_uq[0], w_ukv[0])
    merge_wts = (m_norm_g[0][None, :], w_pa[0].astype(BF16), w_pb[0].astype(BF16), w_o[0].astype(BF16))
    ffn_wts = (norm2_g[0][None, :], w_ffn_gate[0].astype(BF16), w_ffn_up[0].astype(BF16),
               w_ffn_down[0].astype(BF16), final_norm_g[None, :])
    cw, cb = conv_w[0], conv_b[0][None, :]

    hm = jnp.concatenate([jnp.zeros((META_PAD, d), x_prompt.dtype), meta_tokens.astype(x_prompt.dtype)], axis=0)
    cos_m, sin_m = _rope_tables(jnp.arange(MCHUNK) - META_PAD)
    _, k_m, vt_m, mqk_m, mvt_m, _, gt_m, _, _ = _in_proj(hm, cos_m, sin_m, in_wts, MCHUNK, MCHUNK)
    meta_parts = (k_m, vt_m[0], mqk_m, mvt_m[0], gt_m)

    rope = _rope_tables(N_META + jnp.arange(max(x_prompt.shape[1], x_sample.shape[1])))
    y_prompt = _trunk(x_prompt, rope, meta_parts, in_wts, cw, cb, merge_wts, ffn_wts)
    y_sample = _trunk(x_sample, rope, meta_parts, in_wts, cw, cb, merge_wts, ffn_wts)
    return (y_prompt, y_sample)
```

```python
import functools

import jax
import jax.numpy as jnp
from jax import lax
from jax.experimental import pallas as pl
from jax.experimental.pallas import tpu as pltpu

F32 = jnp.float32
BF16 = jnp.bfloat16

EPS = 1e-6
N_META = 16
ROPE_THETA = 10000.0
MLA_HEADS = 8
QK_NOPE = 64
QK_ROPE = 32
V_HEAD = 64
QK_DIM = QK_NOPE + QK_ROPE
Q_LORA = 256
KV_LORA = 128
M_HEADS = 4
M_HEAD_DIM = 128
M_WIDTH = M_HEADS * M_HEAD_DIM
LANES = 128
MCHUNK = 128
LCHUNK = 256
META_PAD = MCHUNK - N_META
HALO = 16
NEG = -1e30
LOG2E = 1.4426950408889634
GROWTH_LIMIT = 64.0
FAST_UNROLL = 16
VMEM_LIMIT = 56 * 1024 * 1024


def _dot(a, b):
    return jnp.dot(a, b, preferred_element_type=F32)


def _dot_nt(a, b):
    return lax.dot_general(a, b, (((1,), (1,)), ((), ())), preferred_element_type=F32)


def _dot_tn(a, b):
    return lax.dot_general(a, b, (((0,), (0,)), ((), ())), preferred_element_type=F32)


def _sigmoid(x):
    return 1.0 / (1.0 + jnp.exp(-x))


def _rms(x, g):
    return x * lax.rsqrt(jnp.mean(x * x, axis=-1, keepdims=True) + EPS) * g


def _const_spec(shape):
    return pl.BlockSpec(shape, lambda *_: (0,) * len(shape))


def _in_proj_kernel(x_ref, tq_ref, ck_ref, sk_ref, n1_ref, w1_ref, b1_ref, wm_ref, bm_ref, wg_ref, bg_ref,
                    qn_ref, kvn_ref, wuq_ref, wuk_ref, wuv_ref, vone_ref, wmv_ref, bmv_ref,
                    q_out, k_out, v_out, mqk_out, mv_out, mo_out, gt_out, sga_out, sgb_out):
    xn = _rms(x_ref[...], n1_ref[...]).astype(BF16)
    tq = tq_ref[...]

    p1 = _dot(xn, w1_ref[...]) + b1_ref[...]
    c_q = p1[:, :Q_LORA]
    c_kv = p1[:, Q_LORA:Q_LORA + KV_LORA]
    kr_plain = p1[:, Q_LORA + KV_LORA:Q_LORA + KV_LORA + LANES]
    kr_rot = p1[:, Q_LORA + KV_LORA + LANES:]
    k_rope = kr_plain * ck_ref[...] + kr_rot * sk_ref[...]

    cqn = (_rms(c_q, qn_ref[...]) * (QK_DIM ** -0.5 * LOG2E)).astype(BF16)
    qa = _dot(cqn, wuq_ref[...])
    ckvn = _rms(c_kv, kvn_ref[...]).astype(BF16)
    ka = _dot(ckvn, wuk_ref[...])
    for h in range(MLA_HEADS):
        sl = slice(h * LANES, (h + 1) * LANES)
        q_out[:, sl] = (qa[:, sl] * tq).astype(BF16)
        k_out[:, sl] = (ka[:, sl] + k_rope).astype(BF16)
    v_out[...] = (_dot_nt(wuv_ref[...], ckvn) + vone_ref[...]).astype(BF16)

    pm = _dot(xn, wm_ref[...]) + bm_ref[...]
    mqk_out[...] = pm[:, :2 * M_WIDTH].astype(BF16)
    mo_out[...] = pm[:, 2 * M_WIDTH:3 * M_WIDTH].astype(BF16)
    gt_out[...] = pm[:, 3 * M_WIDTH:]
    mv_out[...] = (_dot_nt(wmv_ref[...], xn) + bmv_ref[...]).astype(BF16)

    pg = _dot(xn, wg_ref[...]) + bg_ref[...]
    d = sga_out.shape[-1]
    sga_out[...] = _sigmoid(pg[:, :d]).astype(BF16)
    sgb_out[...] = _sigmoid(pg[:, d:]).astype(BF16)


def _in_proj(x2, rope_tabs, wts, tb, seq):
    n, d = x2.shape
    rope = pl.BlockSpec((tb, LANES), lambda i: (i % (seq // tb), 0))
    widths = (MLA_HEADS * LANES, MLA_HEADS * LANES, MLA_HEADS * VROWS, 2 * M_WIDTH, M_WIDTH, M_WIDTH, 2 * LANES, d, d)
    dtypes = (BF16, BF16, BF16, BF16, BF16, BF16, F32, BF16, BF16)
    row = lambda w: pl.BlockSpec((tb, w), lambda i: (i, 0))
    out_specs = [row(w) for w in widths]
    out_shape = [jax.ShapeDtypeStruct((n, w), t) for w, t in zip(widths, dtypes)]
    for o in (2, 4):
        out_specs[o] = pl.BlockSpec((None, widths[o], tb), lambda i: (i, 0, 0))
        out_shape[o] = jax.ShapeDtypeStruct((n // tb, widths[o], tb), BF16)
    return pl.pallas_call(
        _in_proj_kernel,
        grid=(n // tb,),
        in_specs=[row(d)] + [rope] * len(rope_tabs) + [_const_spec(w.shape) for w in wts],
        out_specs=out_specs,
        out_shape=out_shape,
        compiler_params=pltpu.CompilerParams(dimension_semantics=("parallel",), vmem_limit_bytes=VMEM_LIMIT),
        name="in_proj",
    )(x2, *rope_tabs, *wts)


def _attn_kernel(q_ref, k_ref, vt_ref, km_ref, vtm_ref, bias_ref, o_ref,
                 m_scr, acc_scr, g_scr, mx_scr, al_scr, s_scr, p_scr, *, nk):
    hsl = [slice(0, LANES), slice(LANES, 2 * LANES)]
    vsl = [slice(0, VROWS), slice(VROWS, 2 * VROWS)]
    qs = [q_ref[:, sl] for sl in hsl]

    def meta_init():
        for hh, sl in enumerate(hsl):
            s = _dot_nt(km_ref[:, sl], qs[hh]) + bias_ref[...]
            m = jnp.max(s, axis=0, keepdims=True)
            p = jnp.exp2(s - m).astype(BF16)
            m_scr[hh] = m
            acc_scr[hh] = _dot(vtm_ref[vsl[hh], :], p)

    def pipeline(first, step, n_steps, last, unroll):
        first()

        def body(t, carry):
            for u in range(unroll):
                step(unroll * t + u, u % 2)
            return carry

        n_loop = n_steps // unroll
        lax.fori_loop(0, n_loop, body, 0)
        for i in range(n_loop * unroll, n_steps):
            step(i, i % 2)
        last()

    def scores_exp(c, slot):
        for hh, sl in enumerate(hsl):
            s = _dot_nt(k_ref[c, :, sl], qs[hh])
            ref = m_scr[hh]
            p_scr[slot, hh] = jnp.exp2(s - ref).astype(BF16)
            m_new = jnp.maximum(ref, jnp.max(s, axis=0, keepdims=True))
            al_scr[slot, hh] = jnp.exp2(ref - m_new)
            g_scr[hh] = jnp.maximum(g_scr[hh], m_new - ref)
            m_scr[hh] = m_new

    def values_rebase(c, slot):
        for hh, sl in enumerate(hsl):
            acc_scr[hh] = (acc_scr[hh] + _dot(vt_ref[c, vsl[hh], :], p_scr[slot, hh])) * al_scr[slot, hh]

    def fast_step(i, par):
        scores_exp(i + 1, 1 - par)
        values_rebase(i, par)

    meta_init()
    g_scr[...] = jnp.zeros_like(g_scr)
    pipeline(lambda: scores_exp(0, 0), fast_step, nk - 1, lambda: values_rebase(nk - 1, (nk - 1) % 2), FAST_UNROLL)

    def scores(c, slot):
        for hh, sl in enumerate(hsl):
            s = _dot_nt(k_ref[c, :, sl], qs[hh])
            mx_scr[slot, hh] = jnp.max(s, axis=0, keepdims=True)
            s_scr[slot, hh] = s

    def softmax(slot):
        for hh in range(2):
            m_prev = m_scr[hh]
            m_new = jnp.maximum(m_prev, mx_scr[slot, hh])
            al_scr[slot, hh] = jnp.exp2(m_prev - m_new)
            m_scr[hh] = m_new
            p_scr[slot, hh] = jnp.exp2(s_scr[slot, hh] - m_new).astype(BF16)

    def values(c, slot):
        for hh, sl in enumerate(hsl):
            acc_scr[hh] = al_scr[slot, hh] * acc_scr[hh] + _dot(vt_ref[c, vsl[hh], :], p_scr[slot, hh])

    def safe_first():
        scores(0, 0)
        scores(1, 1)
        softmax(0)

    def safe_step(i, par):
        scores(i + 2, par)
        softmax(1 - par)
        values(i, par)

    def safe_last():
        softmax(1)
        values(nk - 2, 0)
        values(nk - 1, 1)

    @pl.when(jnp.max(jnp.maximum(g_scr[0], g_scr[1])) > GROWTH_LIMIT)
    def _():
        meta_init()
        pipeline(safe_first, safe_step, nk - 2, safe_last, 2)

    o_t = [acc_scr[hh][:V_HEAD, :] / acc_scr[hh][V_HEAD:V_HEAD + 1, :] for hh in range(2)]
    o_ref[...] = jnp.concatenate(o_t, axis=0).T.astype(BF16)


def _attention(q, k, vt, km, vtm, bias, tq):
    b, s, _ = q.shape
    nk = k.shape[0] // b
    assert nk >= 2 and nk % 2 == 0, "the key-chunk pipeline is unrolled by two"
    tk = k.shape[1]
    hp = MLA_HEADS // 2
    w2 = 2 * LANES
    return pl.pallas_call(
        functools.partial(_attn_kernel, nk=nk),
        grid=(b, hp, s // tq),
        in_specs=[
            pl.BlockSpec((None, tq, w2), lambda bi, h, i: (bi, i, h)),
            pl.BlockSpec((nk, tk, w2), lambda bi, h, i: (bi, 0, h)),
            pl.BlockSpec((nk, 2 * VROWS, tk), lambda bi, h, i: (bi, h, 0)),
            pl.BlockSpec((MCHUNK, w2), lambda bi, h, i: (0, h)),
            pl.BlockSpec((2 * VROWS, MCHUNK), lambda bi, h, i: (h, 0)),
            pl.BlockSpec((MCHUNK, 1), lambda bi, h, i: (0, 0)),
        ],
        out_specs=pl.BlockSpec((None, tq, LANES), lambda bi, h, i: (bi, i, h)),
        out_shape=jax.ShapeDtypeStruct((b, s, hp * LANES), BF16),
        scratch_shapes=[
            pltpu.VMEM((2, 1, tq), F32), pltpu.VMEM((2, VROWS, tq), F32),
            pltpu.VMEM((2, 1, tq), F32),
            pltpu.VMEM((2, 2, 1, tq), F32), pltpu.VMEM((2, 2, 1, tq), F32),
            pltpu.VMEM((2, 2, tk, tq), F32), pltpu.VMEM((2, 2, tk, tq), BF16),
        ],
        compiler_params=pltpu.CompilerParams(
            dimension_semantics=("parallel", "parallel", "arbitrary"), vmem_limit_bytes=VMEM_LIMIT),
        name="mla_attention",
    )(q, k, vt, km, vtm, bias)


def _conv_silu(cur, prev_row, next_row, cw, cb):
    n = cur.shape[0]
    ridx = lax.broadcasted_iota(jnp.int32, cur.shape, 0)
    x_prev = jnp.where(ridx == 0, prev_row, pltpu.roll(cur, 1, axis=0))
    x_next = jnp.where(ridx == n - 1, next_row, pltpu.roll(cur, n - 1, axis=0))
    y = cw[0:1, :] * x_prev + cw[1:2, :] * cur + cw[2:3, :] * x_next + cb
    return y * _sigmoid(y)


def _log_sigmoid(x):
    return jnp.minimum(x, 0.0) - jnp.log(1.0 + jnp.exp(-jnp.abs(x)))


def _tri(n, upper):
    r = lax.broadcasted_iota(jnp.int32, (n, n), 0)
    c = lax.broadcasted_iota(jnp.int32, (n, n), 1)
    return (c >= r) if upper else (c <= r)


FF_CHUNK = 1024
VROWS = LANES
AUG = 16
CROWS = M_HEAD_DIM + AUG


def _v_aug_t(vt_h):
    r = lax.broadcasted_iota(jnp.int32, (AUG, vt_h.shape[1]), 0)
    return jnp.concatenate([vt_h, jnp.where(r == 0, 1.0, 0.0).astype(vt_h.dtype)], axis=0)


def _gate_rows(gi8, gf8, valid, tri_t, last, m_prev8):
    a8 = _log_sigmoid(gf8)
    b8 = gi8
    if valid is not None:
        a8 = jnp.where(valid, a8, 0.0)
        b8 = jnp.where(valid, b8, NEG)
    f8 = jnp.dot(a8, tri_t, precision=lax.Precision.HIGHEST, preferred_element_type=F32)
    f_tot = f8[:, last:last + 1]
    g8 = f_tot - f8 + b8
    m_new8 = jnp.maximum(f_tot + m_prev8, jnp.max(g8, axis=1, keepdims=True))
    return f8, jnp.exp(g8 - m_new8), jnp.exp(f_tot + m_prev8 - m_new8), m_new8


def _state_update(k_h, vaug_t, w_row, decay, ct_ref, h):
    vw = (vaug_t.astype(F32) * w_row).astype(BF16)
    ct_ref[h] = decay * ct_ref[h] + _dot(vw, k_h)


def _mlstm_chunk(q, k, vt_ref, gc, gr, ct_ref, m_ref, reverse):
    n = q.shape[0]
    r0 = M_HEADS if reverse else 0
    mask_t = _tri(n, not reverse)
    tri_t = mask_t.astype(F32)
    tri_c = _tri(n, reverse).astype(F32)
    last = 0 if reverse else n - 1
    m_prev8 = m_ref[:, 0:1]
    f8, w8, decay8, m_new8 = _gate_rows(gr[0:8, :], gr[8:16, :], None, tri_t, last, m_prev8)
    inter8 = f8 + m_prev8
    c_cols = jnp.dot(tri_c, _log_sigmoid(gc[:, LANES:]), precision=lax.Precision.HIGHEST,
                     preferred_element_type=F32) - gc[:, :LANES]
    outs = []
    for h in range(M_HEADS):
        r = r0 + h
        sl = slice(h * M_HEAD_DIM, (h + 1) * M_HEAD_DIM)
        q_h = q[:, sl].astype(BF16)
        k_h = k[:, sl].astype(BF16)
        vaug_t = _v_aug_t(vt_ref[sl, :])
        inter = inter8[r:r + 1, :]
        log_d = jnp.where(mask_t, f8[r:r + 1, :] - c_cols[:, r:r + 1], NEG)
        m_j = jnp.maximum(inter, jnp.max(log_d, axis=0, keepdims=True))
        s_t = (_dot_nt(k_h, q_h) * jnp.exp(log_d - m_j)).astype(BF16)
        h_t = _dot_nt(ct_ref[h].astype(BF16), q_h) * jnp.exp(inter - m_j) + _dot(vaug_t, s_t)
        den = h_t[M_HEAD_DIM:M_HEAD_DIM + 1, :]
        out_t = h_t[:M_HEAD_DIM, :] / jnp.maximum(jnp.abs(den), jnp.exp(-m_j))
        outs.append(out_t.T)
        _state_update(k_h, vaug_t, w8[r:r + 1, :], decay8[r:r + 1, :], ct_ref, h)
    m_ref[...] = jnp.broadcast_to(m_new8, m_ref.shape)
    return jnp.concatenate(outs, axis=1)


def _mlstm_kernel(qk_f, pv_f, nx_f, vt_f, gc_f, gr_f, qk_b, pv_b, nx_b, vt_b, gc_b, gr_b,
                  qk_m, vt_m, gr_m, cw_ref, cb_ref, hf_out, hb_out, cf_scr, mf_scr, cb_scr, mb_scr):
    j = pl.program_id(1)
    nc = pl.num_programs(1)
    cw = cw_ref[...]
    cb = cb_ref[...]
    scale = M_HEAD_DIM ** -0.5
    meta_last = qk_m[MCHUNK - 1:MCHUNK, :].astype(F32)

    @pl.when(j == 0)
    def _():
        cf_scr[...] = jnp.zeros_like(cf_scr)
        cb_scr[...] = jnp.zeros_like(cb_scr)
        mb_scr[...] = jnp.zeros_like(mb_scr)
        valid_col = lax.broadcasted_iota(jnp.int32, (MCHUNK, 1), 0) >= META_PAD
        valid_row = lax.broadcasted_iota(jnp.int32, (1, MCHUNK), 1) >= META_PAD
        cur = jnp.where(valid_col, qk_m[...].astype(F32), 0.0)
        first_x = qk_f[0:1, :].astype(F32)
        k_all = _conv_silu(cur, jnp.zeros_like(first_x), first_x, cw, cb)[:, M_WIDTH:] * scale
        _, w8, decay8, m_new8 = _gate_rows(gr_m[0:8, :], gr_m[8:16, :], valid_row, _tri(MCHUNK, True).astype(F32),
                                           MCHUNK - 1, jnp.zeros((8, 1), F32))
        for h in range(M_HEADS):
            sl = slice(h * M_HEAD_DIM, (h + 1) * M_HEAD_DIM)
            _state_update(k_all[:, sl].astype(BF16), _v_aug_t(vt_m[sl, :]), w8[h:h + 1, :], decay8[h:h + 1, :],
                          cf_scr, h)
        mf_scr[...] = jnp.broadcast_to(m_new8, mf_scr.shape)

    def run(qk_ref, pv_ref, nx_ref, vt_ref, gc_ref, gr_ref, c, c_scr, m_scr, reverse, out_ref):
        prev_row = jnp.where(c == 0, meta_last, pv_ref[HALO - 1:HALO, :].astype(F32))
        next_row = jnp.where(c == nc - 1, 0.0, nx_ref[0:1, :].astype(F32))
        qk = _conv_silu(qk_ref[...].astype(F32), prev_row, next_row, cw, cb)
        h = _mlstm_chunk(qk[:, :M_WIDTH], qk[:, M_WIDTH:] * scale, vt_ref, gc_ref[...], gr_ref[...],
                         c_scr, m_scr, reverse)
        out_ref[...] = h.astype(out_ref.dtype)

    run(qk_f, pv_f, nx_f, vt_f, gc_f, gr_f, j, cf_scr, mf_scr, False, hf_out)
    run(qk_b, pv_b, nx_b, vt_b, gc_b, gr_b, nc - 1 - j, cb_scr, mb_scr, True, hb_out)


def _mlstm(mqk, mvt, gcol, grow, mqk_m, mvt_m, grow_m, conv_w, conv_b):
    b, s, _ = mqk.shape
    nc = s // LCHUNK
    hpc = LCHUNK // HALO
    nhb = s // HALO
    cps = mvt.shape[2] // LCHUNK
    spb = mvt.shape[0] // b

    def specs(chunk_of):
        return [
            pl.BlockSpec((None, LCHUNK, 2 * M_WIDTH), lambda bi, j: (bi, chunk_of(j), 0)),
            pl.BlockSpec((None, HALO, 2 * M_WIDTH), lambda bi, j: (bi, jnp.maximum(chunk_of(j) * hpc - 1, 0), 0)),
            pl.BlockSpec((None, HALO, 2 * M_WIDTH),
                         lambda bi, j: (bi, jnp.minimum((chunk_of(j) + 1) * hpc, nhb - 1), 0)),
            pl.BlockSpec((None, M_WIDTH, LCHUNK), lambda bi, j: (bi * spb + chunk_of(j) // cps, 0, chunk_of(j) % cps)),
            pl.BlockSpec((None, LCHUNK, 2 * LANES), lambda bi, j: (bi, chunk_of(j), 0)),
            pl.BlockSpec((None, 4 * M_HEADS, LCHUNK), lambda bi, j: (bi, 0, chunk_of(j))),
        ]

    fwd = lambda j: j
    bwd = lambda j: nc - 1 - j
    operands = [mqk, mqk, mqk, mvt, gcol, grow]
    out_spec = lambda chunk_of: pl.BlockSpec((None, LCHUNK, M_WIDTH), lambda bi, j: (bi, chunk_of(j), 0))
    state = [pltpu.VMEM((M_HEADS, CROWS, M_HEAD_DIM), F32), pltpu.VMEM((2 * M_HEADS, LANES), F32)]
    return pl.pallas_call(
        _mlstm_kernel,
        grid=(b, nc),
        in_specs=specs(fwd) + specs(bwd) + [_const_spec(a.shape) for a in (mqk_m, mvt_m, grow_m, conv_w, conv_b)],
        out_specs=[out_spec(fwd), out_spec(bwd)],
        out_shape=[jax.ShapeDtypeStruct((b, s, M_WIDTH), BF16)] * 2,
        scratch_shapes=state + state,
        compiler_params=pltpu.CompilerParams(
            dimension_semantics=("parallel", "arbitrary"), vmem_limit_bytes=VMEM_LIMIT),
        name="mlstm",
    )(*operands, *operands, mqk_m, mvt_m, grow_m, conv_w, conv_b)


def _out_kernel(x_ref, a_ref, hf_ref, hb_ref, mo_ref, sga_ref, sgb_ref,
                mg_ref, wpa_ref, wpb_ref, wo_ref, n2_ref, wg_ref, wu_ref, wd_ref, fn_ref, o_ref):
    h = (hf_ref[...].astype(F32) + hb_ref[...].astype(F32)) * _sigmoid(mo_ref[...].astype(F32))
    mg = mg_ref[...]
    parts = []
    for hd in range(M_HEADS):
        sl = slice(hd * M_HEAD_DIM, (hd + 1) * M_HEAD_DIM)
        parts.append(_rms(h[:, sl], mg[:, sl]).astype(BF16))
    m_out = jnp.concatenate(parts, axis=1)
    merged = (sga_ref[...].astype(F32) * _dot(a_ref[...], wpa_ref[...])
              + sgb_ref[...].astype(F32) * _dot(m_out, wpb_ref[...]))
    y = x_ref[...] + _dot(merged.astype(BF16), wo_ref[...])

    xn = _rms(y, n2_ref[...]).astype(BF16)
    d_ff = wg_ref.shape[1]
    for c0 in range(0, d_ff, FF_CHUNK):
        c1 = min(c0 + FF_CHUNK, d_ff)
        g = _dot(xn, wg_ref[:, c0:c1])
        u = _dot(xn, wu_ref[:, c0:c1])
        y = y + _dot((g * _sigmoid(g) * u).astype(BF16), wd_ref[c0:c1, :])
    o_ref[...] = _rms(y, fn_ref[...])


def _out_proj(x2, a2, hf2, hb2, mo2, sga2, sgb2, wts, tb):
    n, d = x2.shape
    row = lambda w: pl.BlockSpec((tb, w), lambda i: (i, 0))
    const = lambda w: pl.BlockSpec(w.shape, lambda i: (0,) * w.ndim, pipeline_mode=pl.Buffered(1))
    return pl.pallas_call(
        _out_kernel,
        grid=(n // tb,),
        in_specs=[row(d), row(a2.shape[1]), row(M_WIDTH), row(M_WIDTH), row(M_WIDTH), row(d), row(d)]
        + [const(w) for w in wts],
        out_specs=row(d),
        out_shape=jax.ShapeDtypeStruct((n, d), F32),
        compiler_params=pltpu.CompilerParams(dimension_semantics=("parallel",), vmem_limit_bytes=VMEM_LIMIT),
        name="out_proj",
    )(x2, a2, hf2, hb2, mo2, sga2, sgb2, *wts)


def _rope_tables(pos):
    half = QK_ROPE // 2
    freqs = ROPE_THETA ** (-jnp.arange(half, dtype=F32) / half)
    ang = pos.astype(F32)[:, None] * freqs[None, :]
    c, s = jnp.cos(ang), jnp.sin(ang)
    n = pos.shape[0]
    zeros = jnp.zeros((n, QK_NOPE), F32)
    tq = jnp.concatenate([jnp.ones((n, QK_NOPE), F32), c, c, s, s], axis=1)
    ck = jnp.concatenate([zeros, c, c, c, c], axis=1)
    sk = jnp.concatenate([zeros, s, s, s, s], axis=1)
    return tq, ck, sk


def _rot(w):
    half = QK_ROPE // 2
    return jnp.concatenate([-w[..., half:], w[..., :half]], axis=-1)


def _pad_cols(w, left, total):
    return jnp.pad(w, [(0, 0)] * (w.ndim - 1) + [(left, total - left - w.shape[-1])])


def _prep_weights(norm1_g, w_in, b_in, q_norm_g, kv_norm_g, w_uq, w_ukv):
    d = w_in.shape[0]
    o_kr = Q_LORA + KV_LORA
    o_m = o_kr + QK_ROPE
    o_g = o_m + 4 * M_WIDTH
    o_ga = o_g + 4 * M_HEADS

    def seg1(w):
        kr = w[..., o_kr:o_m]
        lead = jnp.zeros(w.shape[:-1] + (QK_NOPE,), w.dtype)
        return jnp.concatenate([w[..., :o_kr], lead, kr, kr, lead, _rot(kr), _rot(kr)], axis=-1)

    def segm(w):
        gates = w[..., o_g:o_ga].reshape(w.shape[:-1] + (4, M_HEADS))
        g_in = jnp.concatenate([gates[..., 0, :], gates[..., 2, :]], axis=-1)
        g_fg = jnp.concatenate([gates[..., 1, :], gates[..., 3, :]], axis=-1)
        return jnp.concatenate([w[..., o_m:o_m + 2 * M_WIDTH], w[..., o_m + 3 * M_WIDTH:o_g],
                                _pad_cols(g_in, 0, LANES), _pad_cols(g_fg, 0, LANES)], axis=-1)

    b2 = b_in[None, :]
    w1, b1 = seg1(w_in).astype(BF16), seg1(b2)
    wm, bm = segm(w_in).astype(BF16), segm(b2)
    wg, bg = w_in[:, o_ga:].astype(BF16), b2[:, o_ga:]

    uq = w_uq.reshape(Q_LORA, MLA_HEADS, QK_DIM)
    wuq = jnp.concatenate([uq, _rot(uq[..., QK_NOPE:])], axis=-1).reshape(Q_LORA, MLA_HEADS * LANES).astype(BF16)
    ukv = w_ukv.reshape(KV_LORA, MLA_HEADS, QK_NOPE + V_HEAD)
    wuk = _pad_cols(ukv[..., :QK_NOPE], 0, LANES).reshape(KV_LORA, MLA_HEADS * LANES).astype(BF16)
    wuv = _pad_cols(ukv[..., QK_NOPE:], 0, VROWS).reshape(KV_LORA, MLA_HEADS * VROWS).T.astype(BF16)
    vone = jnp.tile((jnp.arange(VROWS) == V_HEAD).astype(F32), MLA_HEADS)[:, None]
    o_v = o_m + 2 * M_WIDTH
    wmv, bmv = w_in[:, o_v:o_v + M_WIDTH].T.astype(BF16), b_in[o_v:o_v + M_WIDTH][:, None]
    return (norm1_g[None, :], w1, b1, wm, bm, wg, bg, q_norm_g[None, :], kv_norm_g[None, :], wuq, wuk, wuv, vone,
            wmv, bmv)


def _pick(n, pref):
    return pref if n % pref == 0 else n


def _trunk(x, rope, meta_parts, in_wts, conv_w, conv_b, merge_wts, ffn_wts):
    b, s, d = x.shape
    n = b * s
    tb = _pick(s, 512)
    k_m, vt_m, mqk_m, mvt_m, gt_m = meta_parts

    x2 = x.reshape(n, d)
    q, k, vt, mqk, mvt, mo, gt, sga, sgb = _in_proj(x2, rope, in_wts, tb, s)

    r3 = lambda a: a.reshape(b, s, a.shape[-1])
    bias = jnp.where(jnp.arange(MCHUNK) >= META_PAD, 0.0, NEG).astype(F32)[:, None]
    a_out = _attention(r3(q), k.reshape(n // tb, tb, k.shape[-1]), vt, k_m, vt_m, bias, _pick(s, 512))

    gate_rows = lambda g: jnp.swapaxes(
        jnp.concatenate([g[..., :2 * M_HEADS], g[..., LANES:LANES + 2 * M_HEADS]], axis=-1), -1, -2)
    gt3 = r3(gt)
    h_f, h_b = _mlstm(r3(mqk), mvt, gt3, gate_rows(gt3), mqk_m, mvt_m, gate_rows(gt_m), conv_w, conv_b)

    y = _out_proj(x2, a_out.reshape(n, -1), h_f.reshape(n, -1), h_b.reshape(n, -1), mo, sga, sgb,
                  merge_wts + ffn_wts, tb)
    return y.reshape(b, s, d)


def kernel(x_prompt, x_sample, meta_tokens, norm1_g, w_in, b_in, conv_w, conv_b, q_norm_g, kv_norm_g, w_uq, w_ukv, m_norm_g, w_pa, w_pb, w_o, norm2_g, w_ffn_gate, w_ffn_up, w_ffn_down, final_norm_g):
    assert w_in.shape[0] == 1, "single-layer trunk"
    d = x_prompt.shape[-1]
    in_wts = _prep_weights(norm1_g[0], w_in[0], b_in[0], q_norm_g[0], kv_norm_g[0], w_uq[0], w_ukv[0])
    merge_wts = (m_norm_g[0][None, :], w_pa[0].astype(BF16), w_pb[0].astype(BF16), w_o[0].astype(BF16))
    ffn_wts = (norm2_g[0][None, :], w_ffn_gate[0].astype(BF16), w_ffn_up[0].astype(BF16),
               w_ffn_down[0].astype(BF16), final_norm_g[None, :])
    cw, cb = conv_w[0], conv_b[0][None, :]

    hm = jnp.concatenate([jnp.zeros((META_PAD, d), x_prompt.dtype), meta_tokens.astype(x_prompt.dtype)], axis=0)
    rope_m = _rope_tables(jnp.arange(MCHUNK) - META_PAD)
    _, k_m, vt_m, mqk_m, mvt_m, _, gt_m, _, _ = _in_proj(hm, rope_m, in_wts, MCHUNK, MCHUNK)
    meta_parts = (k_m, vt_m[0], mqk_m, mvt_m[0], gt_m)

    rope = _rope_tables(N_META + jnp.arange(max(x_prompt.shape[1], x_sample.shape[1])))
    y_prompt = _trunk(x_prompt, rope, meta_parts, in_wts, cw, cb, merge_wts, ffn_wts)
    y_sample = _trunk(x_sample, rope, meta_parts, in_wts, cw, cb, merge_wts, ffn_wts)
    return (y_prompt, y_sample)
```

```python
import functools

import jax
import jax.numpy as jnp
from jax import lax
from jax.experimental import pallas as pl
from jax.experimental.pallas import tpu as pltpu

F32 = jnp.float32
BF16 = jnp.bfloat16

EPS = 1e-6
N_META = 16
ROPE_THETA = 10000.0
MLA_HEADS = 8
QK_NOPE = 64
QK_ROPE = 32
V_HEAD = 64
QK_DIM = QK_NOPE + QK_ROPE
Q_LORA = 256
KV_LORA = 128
M_HEADS = 4
M_HEAD_DIM = 128
M_WIDTH = M_HEADS * M_HEAD_DIM
LANES = 128
MCHUNK = 128
LCHUNK = 256
META_PAD = MCHUNK - N_META
HALO = 16
NEG = -1e30
LOG2E = 1.4426950408889634
GROWTH_LIMIT = 64.0
FAST_UNROLL = 16
VMEM_LIMIT = 56 * 1024 * 1024


def _dot(a, b):
    return jnp.dot(a, b, preferred_element_type=F32)


def _dot_nt(a, b):
    return lax.dot_general(a, b, (((1,), (1,)), ((), ())), preferred_element_type=F32)


def _dot_tn(a, b):
    return lax.dot_general(a, b, (((0,), (0,)), ((), ())), preferred_element_type=F32)


def _sigmoid(x):
    return 1.0 / (1.0 + jnp.exp(-x))


def _rms(x, g):
    return x * lax.rsqrt(jnp.mean(x * x, axis=-1, keepdims=True) + EPS) * g


def _const_spec(shape):
    return pl.BlockSpec(shape, lambda *_: (0,) * len(shape))


def _in_proj_kernel(x_ref, cos_ref, sin_ref, n1_ref, w1_ref, b1_ref, wm_ref, bm_ref, wg_ref, bg_ref,
                    qn_ref, kvn_ref, wuq_ref, wuk_ref, wuv_ref, vone_ref, wmv_ref, bmv_ref,
                    q_out, k_out, v_out, mqk_out, mv_out, mo_out, gt_out, sga_out, sgb_out, gr_out):
    xn = _rms(x_ref[...], n1_ref[...]).astype(BF16)
    cos, sin = cos_ref[...], sin_ref[...]
    cos_hi, sin_hi = pltpu.roll(cos, QK_ROPE, axis=1), pltpu.roll(sin, QK_ROPE, axis=1)
    lane = lax.broadcasted_iota(jnp.int32, cos.shape, 1)
    tq = jnp.where(lane < QK_NOPE, 1.0, cos + sin_hi)

    p1 = _dot(xn, w1_ref[...]) + b1_ref[...]
    c_q = p1[:, :Q_LORA]
    c_kv = p1[:, Q_LORA:Q_LORA + KV_LORA]
    kr_plain = p1[:, Q_LORA + KV_LORA:Q_LORA + KV_LORA + LANES]
    kr_rot = p1[:, Q_LORA + KV_LORA + LANES:]
    k_rope = kr_plain * (cos + cos_hi) + kr_rot * (sin + sin_hi)

    cqn = (_rms(c_q, qn_ref[...]) * (QK_DIM ** -0.5 * LOG2E)).astype(BF16)
    qa = _dot(cqn, wuq_ref[...])
    ckvn = _rms(c_kv, kvn_ref[...]).astype(BF16)
    ka = _dot(ckvn, wuk_ref[...])
    for h in range(MLA_HEADS):
        sl = slice(h * LANES, (h + 1) * LANES)
        q_out[:, sl] = (qa[:, sl] * tq).astype(BF16)
        k_out[:, sl] = (ka[:, sl] + k_rope).astype(BF16)
    v_out[...] = (_dot_nt(wuv_ref[...], ckvn) + vone_ref[...]).astype(BF16)

    pm = _dot(xn, wm_ref[...]) + bm_ref[...]
    mqk_out[...] = pm[:, :2 * M_WIDTH].astype(BF16)
    mo_out[...] = pm[:, 2 * M_WIDTH:3 * M_WIDTH].astype(BF16)
    gt_out[...] = pm[:, 3 * M_WIDTH:]
    mt = _dot_nt(wmv_ref[...], xn) + bmv_ref[...]
    mv_out[...] = mt[:M_WIDTH, :].astype(BF16)
    gr_out[...] = mt[M_WIDTH:, :]

    pg = _dot(xn, wg_ref[...]) + bg_ref[...]
    d = sga_out.shape[-1]
    sga_out[...] = _sigmoid(pg[:, :d]).astype(BF16)
    sgb_out[...] = _sigmoid(pg[:, d:]).astype(BF16)


def _in_proj(x2, rope_tabs, wts, tb, seq):
    n, d = x2.shape
    rope = pl.BlockSpec((tb, LANES), lambda i: (i % (seq // tb), 0))
    widths = (MLA_HEADS * LANES, MLA_HEADS * LANES, MLA_HEADS * VROWS, 2 * M_WIDTH, M_WIDTH, M_WIDTH, 2 * LANES, d, d)
    dtypes = (BF16, BF16, BF16, BF16, BF16, BF16, F32, BF16, BF16)
    row = lambda w: pl.BlockSpec((tb, w), lambda i: (i, 0))
    out_specs = [row(w) for w in widths]
    out_shape = [jax.ShapeDtypeStruct((n, w), t) for w, t in zip(widths, dtypes)]
    for o in (2, 4):
        out_specs[o] = pl.BlockSpec((None, widths[o], tb), lambda i: (i, 0, 0))
        out_shape[o] = jax.ShapeDtypeStruct((n // tb, widths[o], tb), BF16)
    out_specs.append(pl.BlockSpec((4 * M_HEADS, tb), lambda i: (0, i)))
    out_shape.append(jax.ShapeDtypeStruct((4 * M_HEADS, n), F32))
    return pl.pallas_call(
        _in_proj_kernel,
        grid=(n // tb,),
        in_specs=[row(d)] + [rope] * len(rope_tabs) + [_const_spec(w.shape) for w in wts],
        out_specs=out_specs,
        out_shape=out_shape,
        compiler_params=pltpu.CompilerParams(dimension_semantics=("parallel",), vmem_limit_bytes=VMEM_LIMIT),
        name="in_proj",
    )(x2, *rope_tabs, *wts)


def _attn_kernel(q_ref, k_ref, vt_ref, km_ref, vtm_ref, bias_ref, o_ref,
                 m_scr, acc_scr, g_scr, mx_scr, al_scr, s_scr, p_scr, *, nk):
    hsl = [slice(0, LANES), slice(LANES, 2 * LANES)]
    vsl = [slice(0, VROWS), slice(VROWS, 2 * VROWS)]
    qs = [q_ref[:, sl] for sl in hsl]

    def meta_init():
        for hh, sl in enumerate(hsl):
            s = _dot_nt(km_ref[:, sl], qs[hh]) + bias_ref[...]
            m = jnp.max(s, axis=0, keepdims=True)
            p = jnp.exp2(s - m).astype(BF16)
            m_scr[hh] = m
            acc_scr[hh] = _dot(vtm_ref[vsl[hh], :], p)

    def pipeline(first, step, n_steps, last, unroll):
        first()

        def body(t, carry):
            for u in range(unroll):
                step(unroll * t + u, u % 2)
            return carry

        n_loop = n_steps // unroll
        lax.fori_loop(0, n_loop, body, 0)
        for i in range(n_loop * unroll, n_steps):
            step(i, i % 2)
        last()

    def scores_exp(c, slot):
        for hh, sl in enumerate(hsl):
            s = _dot_nt(k_ref[c, :, sl], qs[hh])
            ref = m_scr[hh]
            p_scr[slot, hh] = jnp.exp2(s - ref).astype(BF16)
            m_new = jnp.maximum(ref, jnp.max(s, axis=0, keepdims=True))
            al_scr[slot, hh] = jnp.exp2(ref - m_new)
            g_scr[hh] = jnp.maximum(g_scr[hh], m_new - ref)
            m_scr[hh] = m_new

    def values_rebase(c, slot):
        for hh, sl in enumerate(hsl):
            acc_scr[hh] = (acc_scr[hh] + _dot(vt_ref[c, vsl[hh], :], p_scr[slot, hh])) * al_scr[slot, hh]

    def fast_step(i, par):
        scores_exp(i + 1, 1 - par)
        values_rebase(i, par)

    meta_init()
    g_scr[...] = jnp.zeros_like(g_scr)
    pipeline(lambda: scores_exp(0, 0), fast_step, nk - 1, lambda: values_rebase(nk - 1, (nk - 1) % 2), FAST_UNROLL)

    def scores(c, slot):
        for hh, sl in enumerate(hsl):
            s = _dot_nt(k_ref[c, :, sl], qs[hh])
            mx_scr[slot, hh] = jnp.max(s, axis=0, keepdims=True)
            s_scr[slot, hh] = s

    def softmax(slot):
        for hh in range(2):
            m_prev = m_scr[hh]
            m_new = jnp.maximum(m_prev, mx_scr[slot, hh])
            al_scr[slot, hh] = jnp.exp2(m_prev - m_new)
            m_scr[hh] = m_new
            p_scr[slot, hh] = jnp.exp2(s_scr[slot, hh] - m_new).astype(BF16)

    def values(c, slot):
        for hh, sl in enumerate(hsl):
            acc_scr[hh] = al_scr[slot, hh] * acc_scr[hh] + _dot(vt_ref[c, vsl[hh], :], p_scr[slot, hh])

    def safe_first():
        scores(0, 0)
        scores(1, 1)
        softmax(0)

    def safe_step(i, par):
        scores(i + 2, par)
        softmax(1 - par)
        values(i, par)

    def safe_last():
        softmax(1)
        values(nk - 2, 0)
        values(nk - 1, 1)

    @pl.when(jnp.max(jnp.maximum(g_scr[0], g_scr[1])) > GROWTH_LIMIT)
    def _():
        meta_init()
        pipeline(safe_first, safe_step, nk - 2, safe_last, 2)

    o_t = [acc_scr[hh][:V_HEAD, :] / acc_scr[hh][V_HEAD:V_HEAD + 1, :] for hh in range(2)]
    o_ref[...] = jnp.concatenate(o_t, axis=0).T.astype(BF16)


def _attention(q, k, vt, km, vtm, bias, tq):
    b, s, _ = q.shape
    nk = k.shape[0] // b
    assert nk >= 2 and nk % 2 == 0, "the key-chunk pipeline is unrolled by two"
    tk = k.shape[1]
    hp = MLA_HEADS // 2
    w2 = 2 * LANES
    return pl.pallas_call(
        functools.partial(_attn_kernel, nk=nk),
        grid=(b, hp, s // tq),
        in_specs=[
            pl.BlockSpec((None, tq, w2), lambda bi, h, i: (bi, i, h)),
            pl.BlockSpec((nk, tk, w2), lambda bi, h, i: (bi, 0, h)),
            pl.BlockSpec((nk, 2 * VROWS, tk), lambda bi, h, i: (bi, h, 0)),
            pl.BlockSpec((MCHUNK, w2), lambda bi, h, i: (0, h)),
            pl.BlockSpec((2 * VROWS, MCHUNK), lambda bi, h, i: (h, 0)),
            pl.BlockSpec((MCHUNK, 1), lambda bi, h, i: (0, 0)),
        ],
        out_specs=pl.BlockSpec((None, tq, LANES), lambda bi, h, i: (bi, i, h)),
        out_shape=jax.ShapeDtypeStruct((b, s, hp * LANES), BF16),
        scratch_shapes=[
            pltpu.VMEM((2, 1, tq), F32), pltpu.VMEM((2, VROWS, tq), F32),
            pltpu.VMEM((2, 1, tq), F32),
            pltpu.VMEM((2, 2, 1, tq), F32), pltpu.VMEM((2, 2, 1, tq), F32),
            pltpu.VMEM((2, 2, tk, tq), F32), pltpu.VMEM((2, 2, tk, tq), BF16),
        ],
        compiler_params=pltpu.CompilerParams(
            dimension_semantics=("parallel", "parallel", "arbitrary"), vmem_limit_bytes=VMEM_LIMIT),
        name="mla_attention",
    )(q, k, vt, km, vtm, bias)


def _conv_silu(cur, prev_row, next_row, cw, cb):
    n = cur.shape[0]
    ridx = lax.broadcasted_iota(jnp.int32, cur.shape, 0)
    x_prev = jnp.where(ridx == 0, prev_row, pltpu.roll(cur, 1, axis=0))
    x_next = jnp.where(ridx == n - 1, next_row, pltpu.roll(cur, n - 1, axis=0))
    y = cw[0:1, :] * x_prev + cw[1:2, :] * cur + cw[2:3, :] * x_next + cb
    return y * _sigmoid(y)


def _log_sigmoid(x):
    return jnp.minimum(x, 0.0) - jnp.log(1.0 + jnp.exp(-jnp.abs(x)))


def _tri(n, upper):
    r = lax.broadcasted_iota(jnp.int32, (n, n), 0)
    c = lax.broadcasted_iota(jnp.int32, (n, n), 1)
    return (c >= r) if upper else (c <= r)


FF_CHUNK = 1024
VROWS = LANES
AUG = 16
CROWS = M_HEAD_DIM + AUG


def _v_aug_t(vt_h):
    r = lax.broadcasted_iota(jnp.int32, (AUG, vt_h.shape[1]), 0)
    return jnp.concatenate([vt_h, jnp.where(r == 0, 1.0, 0.0).astype(vt_h.dtype)], axis=0)


def _gate_rows(gi8, gf8, valid, tri_t, last, m_prev8):
    a8 = _log_sigmoid(gf8)
    b8 = gi8
    if valid is not None:
        a8 = jnp.where(valid, a8, 0.0)
        b8 = jnp.where(valid, b8, NEG)
    f8 = jnp.dot(a8, tri_t, precision=lax.Precision.HIGHEST, preferred_element_type=F32)
    f_tot = f8[:, last:last + 1]
    g8 = f_tot - f8 + b8
    m_new8 = jnp.maximum(f_tot + m_prev8, jnp.max(g8, axis=1, keepdims=True))
    return f8, jnp.exp(g8 - m_new8), jnp.exp(f_tot + m_prev8 - m_new8), m_new8


def _state_update(k_h, vaug_t, w_row, decay, ct_ref, h):
    vw = (vaug_t.astype(F32) * w_row).astype(BF16)
    ct_ref[h] = decay * ct_ref[h] + _dot(vw, k_h)


def _mlstm_chunk(q, k, vt_ref, gc, gr, ct_ref, m_ref, reverse):
    n = q.shape[0]
    r0 = M_HEADS if reverse else 0
    mask_t = _tri(n, not reverse)
    tri_t = mask_t.astype(F32)
    tri_c = _tri(n, reverse).astype(F32)
    last = 0 if reverse else n - 1
    m_prev8 = m_ref[:, 0:1]
    f8, w8, decay8, m_new8 = _gate_rows(gr[0:8, :], gr[8:16, :], None, tri_t, last, m_prev8)
    inter8 = f8 + m_prev8
    c_cols = jnp.dot(tri_c, _log_sigmoid(gc[:, LANES:]), precision=lax.Precision.HIGHEST,
                     preferred_element_type=F32) - gc[:, :LANES]
    outs = []
    for h in range(M_HEADS):
        r = r0 + h
        sl = slice(h * M_HEAD_DIM, (h + 1) * M_HEAD_DIM)
        q_h = q[:, sl].astype(BF16)
        k_h = k[:, sl].astype(BF16)
        vaug_t = _v_aug_t(vt_ref[sl, :])
        inter = inter8[r:r + 1, :]
        log_d = jnp.where(mask_t, f8[r:r + 1, :] - c_cols[:, r:r + 1], NEG)
        m_j = jnp.maximum(inter, jnp.max(log_d, axis=0, keepdims=True))
        s_t = (_dot_nt(k_h, q_h) * jnp.exp(log_d - m_j)).astype(BF16)
        h_t = _dot_nt(ct_ref[h].astype(BF16), q_h) * jnp.exp(inter - m_j) + _dot(vaug_t, s_t)
        den = h_t[M_HEAD_DIM:M_HEAD_DIM + 1, :]
        out_t = h_t[:M_HEAD_DIM, :] / jnp.maximum(jnp.abs(den), jnp.exp(-m_j))
        outs.append(out_t.T)
        _state_update(k_h, vaug_t, w8[r:r + 1, :], decay8[r:r + 1, :], ct_ref, h)
    m_ref[...] = jnp.broadcast_to(m_new8, m_ref.shape)
    return jnp.concatenate(outs, axis=1)


def _mlstm_kernel(qk_f, pv_f, nx_f, vt_f, gc_f, gr_f, qk_b, pv_b, nx_b, vt_b, gc_b, gr_b,
                  qk_m, vt_m, gr_m, cw_ref, cb_ref, hf_out, hb_out, cf_scr, mf_scr, cb_scr, mb_scr):
    j = pl.program_id(1)
    nc = pl.num_programs(1)
    cw = cw_ref[...]
    cb = cb_ref[...]
    scale = M_HEAD_DIM ** -0.5
    meta_last = qk_m[MCHUNK - 1:MCHUNK, :].astype(F32)

    @pl.when(j == 0)
    def _():
        cf_scr[...] = jnp.zeros_like(cf_scr)
        cb_scr[...] = jnp.zeros_like(cb_scr)
        mb_scr[...] = jnp.zeros_like(mb_scr)
        valid_col = lax.broadcasted_iota(jnp.int32, (MCHUNK, 1), 0) >= META_PAD
        valid_row = lax.broadcasted_iota(jnp.int32, (1, MCHUNK), 1) >= META_PAD
        cur = jnp.where(valid_col, qk_m[...].astype(F32), 0.0)
        first_x = qk_f[0:1, :].astype(F32)
        k_all = _conv_silu(cur, jnp.zeros_like(first_x), first_x, cw, cb)[:, M_WIDTH:] * scale
        _, w8, decay8, m_new8 = _gate_rows(gr_m[0:8, :], gr_m[8:16, :], valid_row, _tri(MCHUNK, True).astype(F32),
                                           MCHUNK - 1, jnp.zeros((8, 1), F32))
        for h in range(M_HEADS):
            sl = slice(h * M_HEAD_DIM, (h + 1) * M_HEAD_DIM)
            _state_update(k_all[:, sl].astype(BF16), _v_aug_t(vt_m[sl, :]), w8[h:h + 1, :], decay8[h:h + 1, :],
                          cf_scr, h)
        mf_scr[...] = jnp.broadcast_to(m_new8, mf_scr.shape)

    def run(qk_ref, pv_ref, nx_ref, vt_ref, gc_ref, gr_ref, c, c_scr, m_scr, reverse, out_ref):
        prev_row = jnp.where(c == 0, meta_last, pv_ref[HALO - 1:HALO, :].astype(F32))
        next_row = jnp.where(c == nc - 1, 0.0, nx_ref[0:1, :].astype(F32))
        qk = _conv_silu(qk_ref[...].astype(F32), prev_row, next_row, cw, cb)
        h = _mlstm_chunk(qk[:, :M_WIDTH], qk[:, M_WIDTH:] * scale, vt_ref, gc_ref[...], gr_ref[...],
                         c_scr, m_scr, reverse)
        out_ref[...] = h.astype(out_ref.dtype)

    run(qk_f, pv_f, nx_f, vt_f, gc_f, gr_f, j, cf_scr, mf_scr, False, hf_out)
    run(qk_b, pv_b, nx_b, vt_b, gc_b, gr_b, nc - 1 - j, cb_scr, mb_scr, True, hb_out)


def _mlstm(mqk, mvt, gcol, grow, mqk_m, mvt_m, grow_m, conv_w, conv_b):
    b, s, _ = mqk.shape
    nc = s // LCHUNK
    hpc = LCHUNK // HALO
    nhb = s // HALO
    cps = mvt.shape[2] // LCHUNK
    spb = mvt.shape[0] // b

    def specs(chunk_of):
        return [
            pl.BlockSpec((None, LCHUNK, 2 * M_WIDTH), lambda bi, j: (bi, chunk_of(j), 0)),
            pl.BlockSpec((None, HALO, 2 * M_WIDTH), lambda bi, j: (bi, jnp.maximum(chunk_of(j) * hpc - 1, 0), 0)),
            pl.BlockSpec((None, HALO, 2 * M_WIDTH),
                         lambda bi, j: (bi, jnp.minimum((chunk_of(j) + 1) * hpc, nhb - 1), 0)),
            pl.BlockSpec((None, M_WIDTH, LCHUNK), lambda bi, j: (bi * spb + chunk_of(j) // cps, 0, chunk_of(j) % cps)),
            pl.BlockSpec((None, LCHUNK, 2 * LANES), lambda bi, j: (bi, chunk_of(j), 0)),
            pl.BlockSpec((4 * M_HEADS, LCHUNK), lambda bi, j: (0, bi * nc + chunk_of(j))),
        ]

    fwd = lambda j: j
    bwd = lambda j: nc - 1 - j
    operands = [mqk, mqk, mqk, mvt, gcol, grow]
    out_spec = lambda chunk_of: pl.BlockSpec((None, LCHUNK, M_WIDTH), lambda bi, j: (bi, chunk_of(j), 0))
    state = [pltpu.VMEM((M_HEADS, CROWS, M_HEAD_DIM), F32), pltpu.VMEM((2 * M_HEADS, LANES), F32)]
    return pl.pallas_call(
        _mlstm_kernel,
        grid=(b, nc),
        in_specs=specs(fwd) + specs(bwd) + [_const_spec(a.shape) for a in (mqk_m, mvt_m, grow_m, conv_w, conv_b)],
        out_specs=[out_spec(fwd), out_spec(bwd)],
        out_shape=[jax.ShapeDtypeStruct((b, s, M_WIDTH), BF16)] * 2,
        scratch_shapes=state + state,
        compiler_params=pltpu.CompilerParams(
            dimension_semantics=("parallel", "arbitrary"), vmem_limit_bytes=VMEM_LIMIT),
        name="mlstm",
    )(*operands, *operands, mqk_m, mvt_m, grow_m, conv_w, conv_b)


def _out_kernel(x_ref, a_ref, hf_ref, hb_ref, mo_ref, sga_ref, sgb_ref,
                mg_ref, wpa_ref, wpb_ref, wo_ref, n2_ref, wg_ref, wu_ref, wd_ref, fn_ref, o_ref):
    h = (hf_ref[...].astype(F32) + hb_ref[...].astype(F32)) * _sigmoid(mo_ref[...].astype(F32))
    mg = mg_ref[...]
    parts = []
    for hd in range(M_HEADS):
        sl = slice(hd * M_HEAD_DIM, (hd + 1) * M_HEAD_DIM)
        parts.append(_rms(h[:, sl], mg[:, sl]).astype(BF16))
    m_out = jnp.concatenate(parts, axis=1)
    merged = (sga_ref[...].astype(F32) * _dot(a_ref[...], wpa_ref[...])
              + sgb_ref[...].astype(F32) * _dot(m_out, wpb_ref[...]))
    y = x_ref[...] + _dot(merged.astype(BF16), wo_ref[...])

    xn = _rms(y, n2_ref[...]).astype(BF16)
    d_ff = wg_ref.shape[1]
    for c0 in range(0, d_ff, FF_CHUNK):
        c1 = min(c0 + FF_CHUNK, d_ff)
        g = _dot(xn, wg_ref[:, c0:c1])
        u = _dot(xn, wu_ref[:, c0:c1])
        y = y + _dot((g * _sigmoid(g) * u).astype(BF16), wd_ref[c0:c1, :])
    o_ref[...] = _rms(y, fn_ref[...])


def _out_proj(x2, a2, hf2, hb2, mo2, sga2, sgb2, wts, tb):
    n, d = x2.shape
    row = lambda w: pl.BlockSpec((tb, w), lambda i: (i, 0))
    const = lambda w: pl.BlockSpec(w.shape, lambda i: (0,) * w.ndim, pipeline_mode=pl.Buffered(1))
    return pl.pallas_call(
        _out_kernel,
        grid=(n // tb,),
        in_specs=[row(d), row(a2.shape[1]), row(M_WIDTH), row(M_WIDTH), row(M_WIDTH), row(d), row(d)]
        + [const(w) for w in wts],
        out_specs=row(d),
        out_shape=jax.ShapeDtypeStruct((n, d), F32),
        compiler_params=pltpu.CompilerParams(dimension_semantics=("parallel",), vmem_limit_bytes=VMEM_LIMIT),
        name="out_proj",
    )(x2, a2, hf2, hb2, mo2, sga2, sgb2, *wts)


def _rope_tables(pos):
    half = QK_ROPE // 2
    freqs = ROPE_THETA ** (-jnp.arange(half, dtype=F32) / half)
    ang = pos.astype(F32)[:, None] * freqs[None, :]
    c, s = jnp.cos(ang), jnp.sin(ang)
    n = pos.shape[0]
    lead = jnp.zeros((n, QK_NOPE), F32)
    tail = jnp.zeros((n, LANES - QK_DIM), F32)
    return jnp.concatenate([lead, c, c, tail], axis=1), jnp.concatenate([lead, s, s, tail], axis=1)


def _rot(w):
    half = QK_ROPE // 2
    return jnp.concatenate([-w[..., half:], w[..., :half]], axis=-1)


def _pad_cols(w, left, total):
    return jnp.pad(w, [(0, 0)] * (w.ndim - 1) + [(left, total - left - w.shape[-1])])


def _prep_weights(norm1_g, w_in, b_in, q_norm_g, kv_norm_g, w_uq, w_ukv):
    d = w_in.shape[0]
    o_kr = Q_LORA + KV_LORA
    o_m = o_kr + QK_ROPE
    o_g = o_m + 4 * M_WIDTH
    o_ga = o_g + 4 * M_HEADS

    def seg1(w):
        kr = w[..., o_kr:o_m]
        lead = jnp.zeros(w.shape[:-1] + (QK_NOPE,), w.dtype)
        return jnp.concatenate([w[..., :o_kr], lead, kr, kr, lead, _rot(kr), _rot(kr)], axis=-1)

    def gates8(w):
        gates = w[..., o_g:o_ga].reshape(w.shape[:-1] + (4, M_HEADS))
        return (jnp.concatenate([gates[..., 0, :], gates[..., 2, :]], axis=-1),
                jnp.concatenate([gates[..., 1, :], gates[..., 3, :]], axis=-1))

    def segm(w):
        g_in, g_fg = gates8(w)
        return jnp.concatenate([w[..., o_m:o_m + 2 * M_WIDTH], w[..., o_m + 3 * M_WIDTH:o_g],
                                _pad_cols(g_in, 0, LANES), _pad_cols(g_fg, 0, LANES)], axis=-1)

    def segv(w):
        return jnp.concatenate([w[..., o_m + 2 * M_WIDTH:o_m + 3 * M_WIDTH], *gates8(w)], axis=-1)

    b2 = b_in[None, :]
    w1, b1 = seg1(w_in).astype(BF16), seg1(b2)
    wm, bm = segm(w_in).astype(BF16), segm(b2)
    wg, bg = w_in[:, o_ga:].astype(BF16), b2[:, o_ga:]

    uq = w_uq.reshape(Q_LORA, MLA_HEADS, QK_DIM)
    wuq = jnp.concatenate([uq, _rot(uq[..., QK_NOPE:])], axis=-1).reshape(Q_LORA, MLA_HEADS * LANES).astype(BF16)
    ukv = w_ukv.reshape(KV_LORA, MLA_HEADS, QK_NOPE + V_HEAD)
    wuk = _pad_cols(ukv[..., :QK_NOPE], 0, LANES).reshape(KV_LORA, MLA_HEADS * LANES).astype(BF16)
    wuv = _pad_cols(ukv[..., QK_NOPE:], 0, VROWS).reshape(KV_LORA, MLA_HEADS * VROWS).T.astype(BF16)
    vone = jnp.tile((jnp.arange(VROWS) == V_HEAD).astype(F32), MLA_HEADS)[:, None]
    wmv, bmv = segv(w_in).T.astype(BF16), segv(b2).T
    return (norm1_g[None, :], w1, b1, wm, bm, wg, bg, q_norm_g[None, :], kv_norm_g[None, :], wuq, wuk, wuv, vone,
            wmv, bmv)


def _pick(n, pref):
    return pref if n % pref == 0 else n


def _trunk(x, rope, meta_parts, in_wts, conv_w, conv_b, merge_wts, ffn_wts):
    b, s, d = x.shape
    n = b * s
    tb = _pick(s, 512)
    k_m, vt_m, mqk_m, mvt_m, gr_m = meta_parts

    x2 = x.reshape(n, d)
    q, k, vt, mqk, mvt, mo, gt, sga, sgb, gr = _in_proj(x2, rope, in_wts, tb, s)

    r3 = lambda a: a.reshape(b, s, a.shape[-1])
    bias = jnp.where(jnp.arange(MCHUNK) >= META_PAD, 0.0, NEG).astype(F32)[:, None]
    a_out = _attention(r3(q), k.reshape(n // tb, tb, k.shape[-1]), vt, k_m, vt_m, bias, _pick(s, 512))

    h_f, h_b = _mlstm(r3(mqk), mvt, r3(gt), gr, mqk_m, mvt_m, gr_m, conv_w, conv_b)

    y = _out_proj(x2, a_out.reshape(n, -1), h_f.reshape(n, -1), h_b.reshape(n, -1), mo, sga, sgb,
                  merge_wts + ffn_wts, tb)
    return y.reshape(b, s, d)


def kernel(x_prompt, x_sample, meta_tokens, norm1_g, w_in, b_in, conv_w, conv_b, q_norm_g, kv_norm_g, w_uq, w_ukv, m_norm_g, w_pa, w_pb, w_o, norm2_g, w_ffn_gate, w_ffn_up, w_ffn_down, final_norm_g):
    assert w_in.shape[0] == 1, "single-layer trunk"
    d = x_prompt.shape[-1]
    in_wts = _prep_weights(norm1_g[0], w_in[0], b_in[0], q_norm_g[0], kv_norm_g[0], w_uq[0], w_ukv[0])
    merge_wts = (m_norm_g[0][None, :], w_pa[0].astype(BF16), w_pb[0].astype(BF16), w_o[0].astype(BF16))
    ffn_wts = (norm2_g[0][None, :], w_ffn_gate[0].astype(BF16), w_ffn_up[0].astype(BF16),
               w_ffn_down[0].astype(BF16), final_norm_g[None, :])
    cw, cb = conv_w[0], conv_b[0][None, :]

    hm = jnp.concatenate([jnp.zeros((META_PAD, d), x_prompt.dtype), meta_tokens.astype(x_prompt.dtype)], axis=0)
    rope_m = _rope_tables(jnp.arange(MCHUNK) - META_PAD)
    _, k_m, vt_m, mqk_m, mvt_m, _, _, _, _, gr_m = _in_proj(hm, rope_m, in_wts, MCHUNK, MCHUNK)
    meta_parts = (k_m, vt_m[0], mqk_m, mvt_m[0], gr_m)

    rope = _rope_tables(N_META + jnp.arange(max(x_prompt.shape[1], x_sample.shape[1])))
    y_prompt = _trunk(x_prompt, rope, meta_parts, in_wts, cw, cb, merge_wts, ffn_wts)
    y_sample = _trunk(x_sample, rope, meta_parts, in_wts, cw, cb, merge_wts, ffn_wts)
    return (y_prompt, y_sample)
```

```python
import functools

import jax
import jax.numpy as jnp
from jax import lax
from jax.experimental import pallas as pl
from jax.experimental.pallas import tpu as pltpu

F32 = jnp.float32
BF16 = jnp.bfloat16

EPS = 1e-6
N_META = 16
ROPE_THETA = 10000.0
MLA_HEADS = 8
QK_NOPE = 64
QK_ROPE = 32
V_HEAD = 64
QK_DIM = QK_NOPE + QK_ROPE
Q_LORA = 256
KV_LORA = 128
M_HEADS = 4
M_HEAD_DIM = 128
M_WIDTH = M_HEADS * M_HEAD_DIM
LANES = 128
MCHUNK = 128
LCHUNK = 256
META_PAD = MCHUNK - N_META
HALO = 16
NEG = -1e30
LOG2E = 1.4426950408889634
GROWTH_LIMIT = 64.0
FAST_UNROLL = 16
VMEM_LIMIT = 56 * 1024 * 1024


def _dot(a, b):
    return jnp.dot(a, b, preferred_element_type=F32)


def _dot_nt(a, b):
    return lax.dot_general(a, b, (((1,), (1,)), ((), ())), preferred_element_type=F32)


def _dot_tn(a, b):
    return lax.dot_general(a, b, (((0,), (0,)), ((), ())), preferred_element_type=F32)


def _sigmoid(x):
    return 1.0 / (1.0 + jnp.exp(-x))


def _rms(x, g):
    return x * lax.rsqrt(jnp.mean(x * x, axis=-1, keepdims=True) + EPS) * g


def _const_spec(shape):
    return pl.BlockSpec(shape, lambda *_: (0,) * len(shape))


def _in_proj_kernel(x_ref, cos_ref, sin_ref, n1_ref, w1_ref, b1_ref, wm_ref, bm_ref, wg_ref, bg_ref,
                    qn_ref, kvn_ref, wuq_ref, wuk_ref, wuv_ref, vone_ref, wmv_ref, bmv_ref,
                    q_out, k_out, v_out, mqk_out, mv_out, mo_out, gt_out, sga_out, sgb_out, gr_out):
    xn = _rms(x_ref[...], n1_ref[...]).astype(BF16)
    cos, sin = cos_ref[...], sin_ref[...]
    cos_hi, sin_hi = pltpu.roll(cos, QK_ROPE, axis=1), pltpu.roll(sin, QK_ROPE, axis=1)
    lane = lax.broadcasted_iota(jnp.int32, cos.shape, 1)
    tq = jnp.where(lane < QK_NOPE, 1.0, cos + sin_hi)

    p1 = _dot(xn, w1_ref[...]) + b1_ref[...]
    c_q = p1[:, :Q_LORA]
    c_kv = p1[:, Q_LORA:Q_LORA + KV_LORA]
    kr_plain = p1[:, Q_LORA + KV_LORA:Q_LORA + KV_LORA + LANES]
    kr_rot = p1[:, Q_LORA + KV_LORA + LANES:]
    k_rope = kr_plain * (cos + cos_hi) + kr_rot * (sin + sin_hi)

    cqn = (_rms(c_q, qn_ref[...]) * (QK_DIM ** -0.5 * LOG2E)).astype(BF16)
    qa = _dot(cqn, wuq_ref[...])
    ckvn = _rms(c_kv, kvn_ref[...]).astype(BF16)
    ka = _dot(ckvn, wuk_ref[...])
    for h in range(MLA_HEADS):
        sl = slice(h * LANES, (h + 1) * LANES)
        q_out[:, sl] = (qa[:, sl] * tq).astype(BF16)
        k_out[:, sl] = (ka[:, sl] + k_rope).astype(BF16)
    v_out[...] = (_dot_nt(wuv_ref[...], ckvn) + vone_ref[...]).astype(BF16)

    pm = _dot(xn, wm_ref[...]) + bm_ref[...]
    mqk_out[...] = pm[:, :2 * M_WIDTH].astype(BF16)
    mo_out[...] = pm[:, 2 * M_WIDTH:3 * M_WIDTH].astype(BF16)
    gt_out[...] = pm[:, 3 * M_WIDTH:]
    mt = _dot_nt(wmv_ref[...], xn) + bmv_ref[...]
    mv_out[...] = mt[:M_WIDTH, :].astype(BF16)
    gr_out[...] = mt[M_WIDTH:, :]

    pg = _dot(xn, wg_ref[...]) + bg_ref[...]
    d = sga_out.shape[-1]
    sga_out[...] = _sigmoid(pg[:, :d]).astype(BF16)
    sgb_out[...] = _sigmoid(pg[:, d:]).astype(BF16)


def _in_proj(x2, rope_tabs, wts, tb, seq):
    n, d = x2.shape
    rope = pl.BlockSpec((tb, LANES), lambda i: (i % (seq // tb), 0))
    widths = (MLA_HEADS * LANES, MLA_HEADS * LANES, MLA_HEADS * VROWS, 2 * M_WIDTH, M_WIDTH, M_WIDTH, 2 * LANES, d, d)
    dtypes = (BF16, BF16, BF16, BF16, BF16, BF16, F32, BF16, BF16)
    row = lambda w: pl.BlockSpec((tb, w), lambda i: (i, 0))
    out_specs = [row(w) for w in widths]
    out_shape = [jax.ShapeDtypeStruct((n, w), t) for w, t in zip(widths, dtypes)]
    for o in (2, 4):
        out_specs[o] = pl.BlockSpec((None, widths[o], tb), lambda i: (i, 0, 0))
        out_shape[o] = jax.ShapeDtypeStruct((n // tb, widths[o], tb), BF16)
    out_specs.append(pl.BlockSpec((4 * M_HEADS, tb), lambda i: (0, i)))
    out_shape.append(jax.ShapeDtypeStruct((4 * M_HEADS, n), F32))
    return pl.pallas_call(
        _in_proj_kernel,
        grid=(n // tb,),
        in_specs=[row(d)] + [rope] * len(rope_tabs) + [_const_spec(w.shape) for w in wts],
        out_specs=out_specs,
        out_shape=out_shape,
        compiler_params=pltpu.CompilerParams(dimension_semantics=("parallel",), vmem_limit_bytes=VMEM_LIMIT),
        name="in_proj",
    )(x2, *rope_tabs, *wts)


def _attn_kernel(q_ref, k_ref, vt_ref, km_ref, vtm_ref, bias_ref, o_ref,
                 m_scr, acc_scr, g_scr, mx_scr, al_scr, s_scr, p_scr, *, nk):
    hsl = [slice(0, LANES), slice(LANES, 2 * LANES)]
    vsl = [slice(0, VROWS), slice(VROWS, 2 * VROWS)]
    qs = [q_ref[:, sl] for sl in hsl]

    def meta_init():
        for hh, sl in enumerate(hsl):
            s = _dot_nt(km_ref[:, sl], qs[hh]) + bias_ref[...]
            m = jnp.max(s, axis=0, keepdims=True)
            p = jnp.exp2(s - m).astype(BF16)
            m_scr[hh] = m
            acc_scr[hh] = _dot(vtm_ref[vsl[hh], :], p)

    def pipeline(first, step, n_steps, last, unroll):
        first()

        def body(t, carry):
            for u in range(unroll):
                step(unroll * t + u, u % 2)
            return carry

        n_loop = n_steps // unroll
        lax.fori_loop(0, n_loop, body, 0)
        for i in range(n_loop * unroll, n_steps):
            step(i, i % 2)
        last()

    def scores_exp(c, slot):
        for hh, sl in enumerate(hsl):
            s = _dot_nt(k_ref[c, :, sl], qs[hh])
            ref = m_scr[hh]
            p_scr[slot, hh] = jnp.exp2(s - ref).astype(BF16)
            m_new = jnp.maximum(ref, jnp.max(s, axis=0, keepdims=True))
            al_scr[slot, hh] = jnp.exp2(ref - m_new)
            g_scr[hh] = jnp.maximum(g_scr[hh], m_new - ref)
            m_scr[hh] = m_new

    def values_rebase(c, slot):
        for hh, sl in enumerate(hsl):
            acc_scr[hh] = (acc_scr[hh] + _dot(vt_ref[c, vsl[hh], :], p_scr[slot, hh])) * al_scr[slot, hh]

    def fast_step(i, par):
        scores_exp(i + 1, 1 - par)
        values_rebase(i, par)

    meta_init()
    g_scr[...] = jnp.zeros_like(g_scr)
    pipeline(lambda: scores_exp(0, 0), fast_step, nk - 1, lambda: values_rebase(nk - 1, (nk - 1) % 2), FAST_UNROLL)

    def scores(c, slot):
        for hh, sl in enumerate(hsl):
            s = _dot_nt(k_ref[c, :, sl], qs[hh])
            mx_scr[slot, hh] = jnp.max(s, axis=0, keepdims=True)
            s_scr[slot, hh] = s

    def softmax(slot):
        for hh in range(2):
            m_prev = m_scr[hh]
            m_new = jnp.maximum(m_prev, mx_scr[slot, hh])
            al_scr[slot, hh] = jnp.exp2(m_prev - m_new)
            m_scr[hh] = m_new
            p_scr[slot, hh] = jnp.exp2(s_scr[slot, hh] - m_new).astype(BF16)

    def values(c, slot):
        for hh, sl in enumerate(hsl):
            acc_scr[hh] = al_scr[slot, hh] * acc_scr[hh] + _dot(vt_ref[c, vsl[hh], :], p_scr[slot, hh])

    def safe_first():
        scores(0, 0)
        scores(1, 1)
        softmax(0)

    def safe_step(i, par):
        scores(i + 2, par)
        softmax(1 - par)
        values(i, par)

    def safe_last():
        softmax(1)
        values(nk - 2, 0)
        values(nk - 1, 1)

    @pl.when(jnp.max(jnp.maximum(g_scr[0], g_scr[1])) > GROWTH_LIMIT)
    def _():
        meta_init()
        pipeline(safe_first, safe_step, nk - 2, safe_last, 2)

    o_t = [acc_scr[hh][:V_HEAD, :] / acc_scr[hh][V_HEAD:V_HEAD + 1, :] for hh in range(2)]
    o_ref[...] = jnp.concatenate(o_t, axis=0).T.astype(BF16)


def _attention(q, k, vt, km, vtm, bias, tq):
    b, s, _ = q.shape
    nk = k.shape[0] // b
    assert nk >= 2 and nk % 2 == 0, "the key-chunk pipeline is unrolled by two"
    tk = k.shape[1]
    hp = MLA_HEADS // 2
    w2 = 2 * LANES
    return pl.pallas_call(
        functools.partial(_attn_kernel, nk=nk),
        grid=(b, hp, s // tq),
        in_specs=[
            pl.BlockSpec((None, tq, w2), lambda bi, h, i: (bi, i, h)),
            pl.BlockSpec((nk, tk, w2), lambda bi, h, i: (bi, 0, h)),
            pl.BlockSpec((nk, 2 * VROWS, tk), lambda bi, h, i: (bi, h, 0)),
            pl.BlockSpec((MCHUNK, w2), lambda bi, h, i: (0, h)),
            pl.BlockSpec((2 * VROWS, MCHUNK), lambda bi, h, i: (h, 0)),
            pl.BlockSpec((MCHUNK, 1), lambda bi, h, i: (0, 0)),
        ],
        out_specs=pl.BlockSpec((None, tq, LANES), lambda bi, h, i: (bi, i, h)),
        out_shape=jax.ShapeDtypeStruct((b, s, hp * LANES), BF16),
        scratch_shapes=[
            pltpu.VMEM((2, 1, tq), F32), pltpu.VMEM((2, VROWS, tq), F32),
            pltpu.VMEM((2, 1, tq), F32),
            pltpu.VMEM((2, 2, 1, tq), F32), pltpu.VMEM((2, 2, 1, tq), F32),
            pltpu.VMEM((2, 2, tk, tq), F32), pltpu.VMEM((2, 2, tk, tq), BF16),
        ],
        compiler_params=pltpu.CompilerParams(
            dimension_semantics=("parallel", "parallel", "arbitrary"), vmem_limit_bytes=VMEM_LIMIT),
        name="mla_attention",
    )(q, k, vt, km, vtm, bias)


def _conv_silu(cur, prev_row, next_row, cw, cb):
    n = cur.shape[0]
    c0, c1, c2 = cw[0:1, :], cw[1:2, :], cw[2:3, :]
    y = c0 * pltpu.roll(cur, 1, axis=0) + c1 * cur + c2 * pltpu.roll(cur, n - 1, axis=0) + cb
    r8 = lax.broadcasted_iota(jnp.int32, (8, cur.shape[1]), 0)
    top = y[0:8, :] + jnp.where(r8 == 0, c0 * (prev_row - cur[n - 1:n, :]), 0.0)
    bot = y[n - 8:n, :] + jnp.where(r8 == 7, c2 * (next_row - cur[0:1, :]), 0.0)
    y = jnp.concatenate([top, y[8:n - 8, :], bot], axis=0)
    return y * _sigmoid(y)


def _log_sigmoid(x):
    return jnp.minimum(x, 0.0) - jnp.log(1.0 + jnp.exp(-jnp.abs(x)))


def _tri(n, upper):
    r = lax.broadcasted_iota(jnp.int32, (n, n), 0)
    c = lax.broadcasted_iota(jnp.int32, (n, n), 1)
    return (c >= r) if upper else (c <= r)


FF_CHUNK = 1024
VROWS = LANES
AUG = 16
CROWS = M_HEAD_DIM + AUG


def _v_aug_t(vt_h):
    r = lax.broadcasted_iota(jnp.int32, (AUG, vt_h.shape[1]), 0)
    return jnp.concatenate([vt_h, jnp.where(r == 0, 1.0, 0.0).astype(vt_h.dtype)], axis=0)


def _split3(a):
    hi = a.astype(BF16)
    r = a - hi.astype(F32)
    mid = r.astype(BF16)
    return hi, mid, (r - mid.astype(F32)).astype(BF16)


def _gate_rows(gi8, gf8, valid, tri_t, last, m_prev8):
    a8 = _log_sigmoid(gf8)
    b8 = gi8
    if valid is not None:
        a8 = jnp.where(valid, a8, 0.0)
        b8 = jnp.where(valid, b8, NEG)
    f8 = sum(_dot(part, tri_t) for part in _split3(a8))
    f_tot = f8[:, last:last + 1]
    g8 = f_tot - f8 + b8
    m_new8 = jnp.maximum(f_tot + m_prev8, jnp.max(g8, axis=1, keepdims=True))
    return f8, jnp.exp(g8 - m_new8), jnp.exp(f_tot + m_prev8 - m_new8), m_new8


def _state_update(k_h, vaug_t, w_row, decay, ct_ref, h):
    vw = (vaug_t.astype(F32) * w_row).astype(BF16)
    ct_ref[h] = decay * ct_ref[h] + _dot(vw, k_h)


def _mlstm_chunk(q, k, vt_ref, gc, gr, ct_ref, m_ref, reverse):
    n = q.shape[0]
    r0 = M_HEADS if reverse else 0
    mask_t = _tri(n, not reverse)
    tri_t = mask_t.astype(BF16)
    tri_c = _tri(n, reverse).astype(BF16)
    last = 0 if reverse else n - 1
    m_prev8 = m_ref[:, 0:1]
    f8, w8, decay8, m_new8 = _gate_rows(gr[0:8, :], gr[8:16, :], None, tri_t, last, m_prev8)
    inter8 = f8 + m_prev8
    c_cols = sum(_dot(tri_c, part) for part in _split3(_log_sigmoid(gc[:, LANES:]))) - gc[:, :LANES]
    outs = []
    for h in range(M_HEADS):
        r = r0 + h
        sl = slice(h * M_HEAD_DIM, (h + 1) * M_HEAD_DIM)
        q_h = q[:, sl].astype(BF16)
        k_h = k[:, sl].astype(BF16)
        vaug_t = _v_aug_t(vt_ref[sl, :])
        inter = inter8[r:r + 1, :]
        log_d = jnp.where(mask_t, f8[r:r + 1, :] - c_cols[:, r:r + 1], NEG)
        m_j = jnp.maximum(inter, jnp.max(log_d, axis=0, keepdims=True))
        s_t = (_dot_nt(k_h, q_h) * jnp.exp(log_d - m_j)).astype(BF16)
        h_t = _dot_nt(ct_ref[h].astype(BF16), q_h) * jnp.exp(inter - m_j) + _dot(vaug_t, s_t)
        den = h_t[M_HEAD_DIM:M_HEAD_DIM + 1, :]
        out_t = h_t[:M_HEAD_DIM, :] / jnp.maximum(jnp.abs(den), jnp.exp(-m_j))
        outs.append(out_t.T)
        _state_update(k_h, vaug_t, w8[r:r + 1, :], decay8[r:r + 1, :], ct_ref, h)
    m_ref[...] = jnp.broadcast_to(m_new8, m_ref.shape)
    return jnp.concatenate(outs, axis=1)


def _mlstm_kernel(qk_f, pv_f, nx_f, vt_f, gc_f, gr_f, qk_b, pv_b, nx_b, vt_b, gc_b, gr_b,
                  qk_m, vt_m, gr_m, cw_ref, cb_ref, hf_out, hb_out, cf_scr, mf_scr, cb_scr, mb_scr):
    j = pl.program_id(1)
    nc = pl.num_programs(1)
    cw = cw_ref[...]
    cb = cb_ref[...]
    scale = M_HEAD_DIM ** -0.5
    meta_last = qk_m[MCHUNK - 1:MCHUNK, :].astype(F32)

    @pl.when(j == 0)
    def _():
        cf_scr[...] = jnp.zeros_like(cf_scr)
        cb_scr[...] = jnp.zeros_like(cb_scr)
        mb_scr[...] = jnp.zeros_like(mb_scr)
        valid_col = lax.broadcasted_iota(jnp.int32, (MCHUNK, 1), 0) >= META_PAD
        valid_row = lax.broadcasted_iota(jnp.int32, (1, MCHUNK), 1) >= META_PAD
        cur = jnp.where(valid_col, qk_m[...].astype(F32), 0.0)
        first_x = qk_f[0:1, :].astype(F32)
        k_all = _conv_silu(cur, jnp.zeros_like(first_x), first_x, cw, cb)[:, M_WIDTH:] * scale
        _, w8, decay8, m_new8 = _gate_rows(gr_m[0:8, :], gr_m[8:16, :], valid_row, _tri(MCHUNK, True).astype(BF16),
                                           MCHUNK - 1, jnp.zeros((8, 1), F32))
        for h in range(M_HEADS):
            sl = slice(h * M_HEAD_DIM, (h + 1) * M_HEAD_DIM)
            _state_update(k_all[:, sl].astype(BF16), _v_aug_t(vt_m[sl, :]), w8[h:h + 1, :], decay8[h:h + 1, :],
                          cf_scr, h)
        mf_scr[...] = jnp.broadcast_to(m_new8, mf_scr.shape)

    def run(qk_ref, pv_ref, nx_ref, vt_ref, gc_ref, gr_ref, c, c_scr, m_scr, reverse, out_ref):
        prev_row = jnp.where(c == 0, meta_last, pv_ref[HALO - 1:HALO, :].astype(F32))
        next_row = jnp.where(c == nc - 1, 0.0, nx_ref[0:1, :].astype(F32))
        qk = _conv_silu(qk_ref[...].astype(F32), prev_row, next_row, cw, cb)
        h = _mlstm_chunk(qk[:, :M_WIDTH], qk[:, M_WIDTH:] * scale, vt_ref, gc_ref[...], gr_ref[...],
                         c_scr, m_scr, reverse)
        out_ref[...] = h.astype(out_ref.dtype)

    run(qk_f, pv_f, nx_f, vt_f, gc_f, gr_f, j, cf_scr, mf_scr, False, hf_out)
    run(qk_b, pv_b, nx_b, vt_b, gc_b, gr_b, nc - 1 - j, cb_scr, mb_scr, True, hb_out)


def _mlstm(mqk, mvt, gcol, grow, mqk_m, mvt_m, grow_m, conv_w, conv_b):
    b, s, _ = mqk.shape
    nc = s // LCHUNK
    hpc = LCHUNK // HALO
    nhb = s // HALO
    cps = mvt.shape[2] // LCHUNK
    spb = mvt.shape[0] // b

    def specs(chunk_of):
        return [
            pl.BlockSpec((None, LCHUNK, 2 * M_WIDTH), lambda bi, j: (bi, chunk_of(j), 0)),
            pl.BlockSpec((None, HALO, 2 * M_WIDTH), lambda bi, j: (bi, jnp.maximum(chunk_of(j) * hpc - 1, 0), 0)),
            pl.BlockSpec((None, HALO, 2 * M_WIDTH),
                         lambda bi, j: (bi, jnp.minimum((chunk_of(j) + 1) * hpc, nhb - 1), 0)),
            pl.BlockSpec((None, M_WIDTH, LCHUNK), lambda bi, j: (bi * spb + chunk_of(j) // cps, 0, chunk_of(j) % cps)),
            pl.BlockSpec((None, LCHUNK, 2 * LANES), lambda bi, j: (bi, chunk_of(j), 0)),
            pl.BlockSpec((4 * M_HEADS, LCHUNK), lambda bi, j: (0, bi * nc + chunk_of(j))),
        ]

    fwd = lambda j: j
    bwd = lambda j: nc - 1 - j
    operands = [mqk, mqk, mqk, mvt, gcol, grow]
    out_spec = lambda chunk_of: pl.BlockSpec((None, LCHUNK, M_WIDTH), lambda bi, j: (bi, chunk_of(j), 0))
    state = [pltpu.VMEM((M_HEADS, CROWS, M_HEAD_DIM), F32), pltpu.VMEM((2 * M_HEADS, LANES), F32)]
    return pl.pallas_call(
        _mlstm_kernel,
        grid=(b, nc),
        in_specs=specs(fwd) + specs(bwd) + [_const_spec(a.shape) for a in (mqk_m, mvt_m, grow_m, conv_w, conv_b)],
        out_specs=[out_spec(fwd), out_spec(bwd)],
        out_shape=[jax.ShapeDtypeStruct((b, s, M_WIDTH), BF16)] * 2,
        scratch_shapes=state + state,
        compiler_params=pltpu.CompilerParams(
            dimension_semantics=("parallel", "arbitrary"), vmem_limit_bytes=VMEM_LIMIT),
        name="mlstm",
    )(*operands, *operands, mqk_m, mvt_m, grow_m, conv_w, conv_b)


def _out_kernel(x_ref, a_ref, hf_ref, hb_ref, mo_ref, sga_ref, sgb_ref,
                mg_ref, wpa_ref, wpb_ref, wo_ref, n2_ref, wg_ref, wu_ref, wd_ref, fn_ref, o_ref):
    h = (hf_ref[...].astype(F32) + hb_ref[...].astype(F32)) * _sigmoid(mo_ref[...].astype(F32))
    mg = mg_ref[...]
    parts = []
    for hd in range(M_HEADS):
        sl = slice(hd * M_HEAD_DIM, (hd + 1) * M_HEAD_DIM)
        parts.append(_rms(h[:, sl], mg[:, sl]).astype(BF16))
    m_out = jnp.concatenate(parts, axis=1)
    merged = (sga_ref[...].astype(F32) * _dot(a_ref[...], wpa_ref[...])
              + sgb_ref[...].astype(F32) * _dot(m_out, wpb_ref[...]))
    y = x_ref[...] + _dot(merged.astype(BF16), wo_ref[...])

    xn = _rms(y, n2_ref[...]).astype(BF16)
    d_ff = wg_ref.shape[1]
    for c0 in range(0, d_ff, FF_CHUNK):
        c1 = min(c0 + FF_CHUNK, d_ff)
        g = _dot(xn, wg_ref[:, c0:c1])
        u = _dot(xn, wu_ref[:, c0:c1])
        y = y + _dot((g * _sigmoid(g) * u).astype(BF16), wd_ref[c0:c1, :])
    o_ref[...] = _rms(y, fn_ref[...])


def _out_proj(x2, a2, hf2, hb2, mo2, sga2, sgb2, wts, tb):
    n, d = x2.shape
    row = lambda w: pl.BlockSpec((tb, w), lambda i: (i, 0))
    const = lambda w: pl.BlockSpec(w.shape, lambda i: (0,) * w.ndim, pipeline_mode=pl.Buffered(1))
    return pl.pallas_call(
        _out_kernel,
        grid=(n // tb,),
        in_specs=[row(d), row(a2.shape[1]), row(M_WIDTH), row(M_WIDTH), row(M_WIDTH), row(d), row(d)]
        + [const(w) for w in wts],
        out_specs=row(d),
        out_shape=jax.ShapeDtypeStruct((n, d), F32),
        compiler_params=pltpu.CompilerParams(dimension_semantics=("parallel",), vmem_limit_bytes=VMEM_LIMIT),
        name="out_proj",
    )(x2, a2, hf2, hb2, mo2, sga2, sgb2, *wts)


def _rope_tables(pos):
    half = QK_ROPE // 2
    n = pos.shape[0]
    freqs = ROPE_THETA ** (-jnp.arange(half, dtype=F32) / half)
    per_row = LANES // half
    ang = jnp.repeat(pos.astype(F32).reshape(n // per_row, per_row), half, axis=1) * jnp.tile(freqs, per_row)[None, :]
    c, s = jnp.cos(ang).reshape(n, half), jnp.sin(ang).reshape(n, half)
    lead = jnp.zeros((n, QK_NOPE), F32)
    tail = jnp.zeros((n, LANES - QK_DIM), F32)
    return jnp.concatenate([lead, c, c, tail], axis=1), jnp.concatenate([lead, s, s, tail], axis=1)


def _rot(w):
    half = QK_ROPE // 2
    return jnp.concatenate([-w[..., half:], w[..., :half]], axis=-1)


def _pad_cols(w, left, total):
    return jnp.pad(w, [(0, 0)] * (w.ndim - 1) + [(left, total - left - w.shape[-1])])


def _prep_weights(norm1_g, w_in, b_in, q_norm_g, kv_norm_g, w_uq, w_ukv):
    d = w_in.shape[0]
    o_kr = Q_LORA + KV_LORA
    o_m = o_kr + QK_ROPE
    o_g = o_m + 4 * M_WIDTH
    o_ga = o_g + 4 * M_HEADS

    def seg1(w):
        kr = w[..., o_kr:o_m]
        lead = jnp.zeros(w.shape[:-1] + (QK_NOPE,), w.dtype)
        return jnp.concatenate([w[..., :o_kr], lead, kr, kr, lead, _rot(kr), _rot(kr)], axis=-1)

    def gates8(w):
        gates = w[..., o_g:o_ga].reshape(w.shape[:-1] + (4, M_HEADS))
        return (jnp.concatenate([gates[..., 0, :], gates[..., 2, :]], axis=-1),
                jnp.concatenate([gates[..., 1, :], gates[..., 3, :]], axis=-1))

    def segm(w):
        g_in, g_fg = gates8(w)
        return jnp.concatenate([w[..., o_m:o_m + 2 * M_WIDTH], w[..., o_m + 3 * M_WIDTH:o_g],
                                _pad_cols(g_in, 0, LANES), _pad_cols(g_fg, 0, LANES)], axis=-1)

    def segv(w):
        return jnp.concatenate([w[..., o_m + 2 * M_WIDTH:o_m + 3 * M_WIDTH], *gates8(w)], axis=-1)

    b2 = b_in[None, :]
    w1, b1 = seg1(w_in).astype(BF16), seg1(b2)
    wm, bm = segm(w_in).astype(BF16), segm(b2)
    wg, bg = w_in[:, o_ga:].astype(BF16), b2[:, o_ga:]

    uq = w_uq.reshape(Q_LORA, MLA_HEADS, QK_DIM)
    wuq = jnp.concatenate([uq, _rot(uq[..., QK_NOPE:])], axis=-1).reshape(Q_LORA, MLA_HEADS * LANES).astype(BF16)
    ukv = w_ukv.reshape(KV_LORA, MLA_HEADS, QK_NOPE + V_HEAD)
    wuk = _pad_cols(ukv[..., :QK_NOPE], 0, LANES).reshape(KV_LORA, MLA_HEADS * LANES).astype(BF16)
    wuv = _pad_cols(ukv[..., QK_NOPE:], 0, VROWS).reshape(KV_LORA, MLA_HEADS * VROWS).T.astype(BF16)
    vone = jnp.tile((jnp.arange(VROWS) == V_HEAD).astype(F32), MLA_HEADS)[:, None]
    wmv, bmv = segv(w_in).T.astype(BF16), segv(b2).T
    return (norm1_g[None, :], w1, b1, wm, bm, wg, bg, q_norm_g[None, :], kv_norm_g[None, :], wuq, wuk, wuv, vone,
            wmv, bmv)


def _pick(n, pref):
    return pref if n % pref == 0 else n


def _trunk(x, rope, meta_parts, in_wts, conv_w, conv_b, merge_wts, ffn_wts):
    b, s, d = x.shape
    n = b * s
    tb = _pick(s, 512)
    k_m, vt_m, mqk_m, mvt_m, gr_m = meta_parts

    x2 = x.reshape(n, d)
    q, k, vt, mqk, mvt, mo, gt, sga, sgb, gr = _in_proj(x2, rope, in_wts, tb, s)

    r3 = lambda a: a.reshape(b, s, a.shape[-1])
    bias = jnp.where(jnp.arange(MCHUNK) >= META_PAD, 0.0, NEG).astype(F32)[:, None]
    a_out = _attention(r3(q), k.reshape(n // tb, tb, k.shape[-1]), vt, k_m, vt_m, bias, _pick(s, 512))

    h_f, h_b = _mlstm(r3(mqk), mvt, r3(gt), gr, mqk_m, mvt_m, gr_m, conv_w, conv_b)

    y = _out_proj(x2, a_out.reshape(n, -1), h_f.reshape(n, -1), h_b.reshape(n, -1), mo, sga, sgb,
                  merge_wts + ffn_wts, tb)
    return y.reshape(b, s, d)


def kernel(x_prompt, x_sample, meta_tokens, norm1_g, w_in, b_in, conv_w, conv_b, q_norm_g, kv_norm_g, w_uq, w_ukv, m_norm_g, w_pa, w_pb, w_o, norm2_g, w_ffn_gate, w_ffn_up, w_ffn_down, final_norm_g):
    assert w_in.shape[0] == 1, "single-layer trunk"
    d = x_prompt.shape[-1]
    in_wts = _prep_weights(norm1_g[0], w_in[0], b_in[0], q_norm_g[0], kv_norm_g[0], w_uq[0], w_ukv[0])
    merge_wts = (m_norm_g[0][None, :], w_pa[0].astype(BF16), w_pb[0].astype(BF16), w_o[0].astype(BF16))
    ffn_wts = (norm2_g[0][None, :], w_ffn_gate[0].astype(BF16), w_ffn_up[0].astype(BF16),
               w_ffn_down[0].astype(BF16), final_norm_g[None, :])
    cw, cb = conv_w[0], conv_b[0][None, :]

    hm = jnp.concatenate([jnp.zeros((META_PAD, d), x_prompt.dtype), meta_tokens.astype(x_prompt.dtype)], axis=0)
    rope_m = _rope_tables(jnp.arange(MCHUNK) - META_PAD)
    _, k_m, vt_m, mqk_m, mvt_m, _, _, _, _, gr_m = _in_proj(hm, rope_m, in_wts, MCHUNK, MCHUNK)
    meta_parts = (k_m, vt_m[0], mqk_m, mvt_m[0], gr_m)

    rope = _rope_tables(N_META + jnp.arange(max(x_prompt.shape[1], x_sample.shape[1])))
    y_prompt = _trunk(x_prompt, rope, meta_parts, in_wts, cw, cb, merge_wts, ffn_wts)
    y_sample = _trunk(x_sample, rope, meta_parts, in_wts, cw, cb, merge_wts, ffn_wts)
    return (y_prompt, y_sample)
```

```python
import functools

import jax
import jax.numpy as jnp
from jax import lax
from jax.experimental import pallas as pl
from jax.experimental.pallas import tpu as pltpu

F32 = jnp.float32
BF16 = jnp.bfloat16

EPS = 1e-6
N_META = 16
ROPE_THETA = 10000.0
MLA_HEADS = 8
QK_NOPE = 64
QK_ROPE = 32
V_HEAD = 64
QK_DIM = QK_NOPE + QK_ROPE
Q_LORA = 256
KV_LORA = 128
M_HEADS = 4
M_HEAD_DIM = 128
M_WIDTH = M_HEADS * M_HEAD_DIM
LANES = 128
MCHUNK = 128
LCHUNK = 256
META_PAD = MCHUNK - N_META
HALO = 16
NEG = -1e30
LOG2E = 1.4426950408889634
GROWTH_LIMIT = 64.0
ATTN_STEP_SCORES = 512 * 16384
FAST_UNROLL = 16
VMEM_LIMIT = 56 * 1024 * 1024


def _dot(a, b):
    return jnp.dot(a, b, preferred_element_type=F32)


def _dot_nt(a, b):
    return lax.dot_general(a, b, (((1,), (1,)), ((), ())), preferred_element_type=F32)


def _dot_tn(a, b):
    return lax.dot_general(a, b, (((0,), (0,)), ((), ())), preferred_element_type=F32)


def _sigmoid(x):
    return 1.0 / (1.0 + jnp.exp(-x))


def _rms(x, g):
    return x * lax.rsqrt(jnp.mean(x * x, axis=-1, keepdims=True) + EPS) * g


def _const_spec(shape):
    return pl.BlockSpec(shape, lambda *_: (0,) * len(shape))


def _in_proj_kernel(x_ref, cos_ref, sin_ref, n1_ref, w1_ref, b1_ref, wm_ref, bm_ref, wg_ref, bg_ref,
                    qn_ref, kvn_ref, wuq_ref, wuk_ref, wuv_ref, vone_ref, wmv_ref, bmv_ref,
                    q_out, k_out, v_out, mqk_out, mv_out, mo_out, gt_out, sga_out, sgb_out, gr_out):
    xn = _rms(x_ref[...], n1_ref[...]).astype(BF16)
    cos, sin = cos_ref[...], sin_ref[...]
    cos_hi, sin_hi = pltpu.roll(cos, QK_ROPE, axis=1), pltpu.roll(sin, QK_ROPE, axis=1)
    lane = lax.broadcasted_iota(jnp.int32, cos.shape, 1)
    tq = jnp.where(lane < QK_NOPE, 1.0, cos + sin_hi)

    p1 = _dot(xn, w1_ref[...]) + b1_ref[...]
    c_q = p1[:, :Q_LORA]
    c_kv = p1[:, Q_LORA:Q_LORA + KV_LORA]
    kr_plain = p1[:, Q_LORA + KV_LORA:Q_LORA + KV_LORA + LANES]
    kr_rot = p1[:, Q_LORA + KV_LORA + LANES:]
    k_rope = kr_plain * (cos + cos_hi) + kr_rot * (sin + sin_hi)

    cqn = (_rms(c_q, qn_ref[...]) * (QK_DIM ** -0.5 * LOG2E)).astype(BF16)
    qa = _dot(cqn, wuq_ref[...])
    ckvn = _rms(c_kv, kvn_ref[...]).astype(BF16)
    ka = _dot(ckvn, wuk_ref[...])
    for h in range(MLA_HEADS):
        sl = slice(h * LANES, (h + 1) * LANES)
        q_out[:, sl] = (qa[:, sl] * tq).astype(BF16)
        k_out[:, sl] = (ka[:, sl] + k_rope).astype(BF16)
    v_out[...] = (_dot_nt(wuv_ref[...], ckvn) + vone_ref[...]).astype(BF16)

    pm = _dot(xn, wm_ref[...]) + bm_ref[...]
    mqk_out[...] = pm[:, :2 * M_WIDTH].astype(BF16)
    mo_out[...] = pm[:, 2 * M_WIDTH:3 * M_WIDTH].astype(BF16)
    gt_out[...] = pm[:, 3 * M_WIDTH:]
    mt = _dot_nt(wmv_ref[...], xn) + bmv_ref[...]
    mv_out[...] = mt[:M_WIDTH, :].astype(BF16)
    gr_out[...] = mt[M_WIDTH:, :]

    pg = _dot(xn, wg_ref[...]) + bg_ref[...]
    d = sga_out.shape[-1]
    sga_out[...] = _sigmoid(pg[:, :d]).astype(BF16)
    sgb_out[...] = _sigmoid(pg[:, d:]).astype(BF16)


def _in_proj(x2, rope_tabs, wts, tb, seq):
    n, d = x2.shape
    rope = pl.BlockSpec((tb, LANES), lambda i: (i % (seq // tb), 0))
    widths = (MLA_HEADS * LANES, MLA_HEADS * LANES, MLA_HEADS * VROWS, 2 * M_WIDTH, M_WIDTH, M_WIDTH, 2 * LANES, d, d)
    dtypes = (BF16, BF16, BF16, BF16, BF16, BF16, F32, BF16, BF16)
    row = lambda w: pl.BlockSpec((tb, w), lambda i: (i, 0))
    out_specs = [row(w) for w in widths]
    out_shape = [jax.ShapeDtypeStruct((n, w), t) for w, t in zip(widths, dtypes)]
    for o in (2, 4):
        out_specs[o] = pl.BlockSpec((None, widths[o], tb), lambda i: (i, 0, 0))
        out_shape[o] = jax.ShapeDtypeStruct((n // tb, widths[o], tb), BF16)
    out_specs.append(pl.BlockSpec((4 * M_HEADS, tb), lambda i: (0, i)))
    out_shape.append(jax.ShapeDtypeStruct((4 * M_HEADS, n), F32))
    return pl.pallas_call(
        _in_proj_kernel,
        grid=(n // tb,),
        in_specs=[row(d)] + [rope] * len(rope_tabs) + [_const_spec(w.shape) for w in wts],
        out_specs=out_specs,
        out_shape=out_shape,
        compiler_params=pltpu.CompilerParams(dimension_semantics=("parallel",), vmem_limit_bytes=VMEM_LIMIT),
        name="in_proj",
    )(x2, *rope_tabs, *wts)


def _attn_kernel(q_ref, k_ref, vt_ref, km_ref, vtm_ref, bias_ref, o_ref,
                 m_scr, acc_scr, g_scr, mx_scr, al_scr, s_scr, p_scr, *, nk):
    hsl = [slice(0, LANES), slice(LANES, 2 * LANES)]
    vsl = [slice(0, VROWS), slice(VROWS, 2 * VROWS)]
    qs = [q_ref[:, sl] for sl in hsl]

    def meta_init():
        for hh, sl in enumerate(hsl):
            s = _dot_nt(km_ref[:, sl], qs[hh]) + bias_ref[...]
            m = jnp.max(s, axis=0, keepdims=True)
            p = jnp.exp2(s - m).astype(BF16)
            m_scr[hh] = m
            acc_scr[hh] = _dot(vtm_ref[vsl[hh], :], p)

    def pipeline(first, step, n_steps, last, unroll):
        first()

        def body(t, carry):
            for u in range(unroll):
                step(unroll * t + u, u % 2)
            return carry

        n_loop = n_steps // unroll
        lax.fori_loop(0, n_loop, body, 0)
        for i in range(n_loop * unroll, n_steps):
            step(i, i % 2)
        last()

    def scores_exp(c, slot):
        for hh, sl in enumerate(hsl):
            s = _dot_nt(k_ref[c, :, sl], qs[hh])
            ref = m_scr[hh]
            p_scr[slot, hh] = jnp.exp2(s - ref).astype(BF16)
            m_new = jnp.maximum(ref, jnp.max(s, axis=0, keepdims=True))
            al_scr[slot, hh] = jnp.exp2(ref - m_new)
            g_scr[hh] = jnp.maximum(g_scr[hh], m_new - ref)
            m_scr[hh] = m_new

    def values_rebase(c, slot):
        for hh, sl in enumerate(hsl):
            acc_scr[hh] = (acc_scr[hh] + _dot(vt_ref[c, vsl[hh], :], p_scr[slot, hh])) * al_scr[slot, hh]

    def fast_step(i, par):
        scores_exp(i + 1, 1 - par)
        values_rebase(i, par)

    meta_init()
    g_scr[...] = jnp.zeros_like(g_scr)
    pipeline(lambda: scores_exp(0, 0), fast_step, nk - 1, lambda: values_rebase(nk - 1, (nk - 1) % 2), FAST_UNROLL)

    def scores(c, slot):
        for hh, sl in enumerate(hsl):
            s = _dot_nt(k_ref[c, :, sl], qs[hh])
            mx_scr[slot, hh] = jnp.max(s, axis=0, keepdims=True)
            s_scr[slot, hh] = s

    def softmax(slot):
        for hh in range(2):
            m_prev = m_scr[hh]
            m_new = jnp.maximum(m_prev, mx_scr[slot, hh])
            al_scr[slot, hh] = jnp.exp2(m_prev - m_new)
            m_scr[hh] = m_new
            p_scr[slot, hh] = jnp.exp2(s_scr[slot, hh] - m_new).astype(BF16)

    def values(c, slot):
        for hh, sl in enumerate(hsl):
            acc_scr[hh] = al_scr[slot, hh] * acc_scr[hh] + _dot(vt_ref[c, vsl[hh], :], p_scr[slot, hh])

    def safe_first():
        scores(0, 0)
        scores(1, 1)
        softmax(0)

    def safe_step(i, par):
        scores(i + 2, par)
        softmax(1 - par)
        values(i, par)

    def safe_last():
        softmax(1)
        values(nk - 2, 0)
        values(nk - 1, 1)

    @pl.when(jnp.max(jnp.maximum(g_scr[0], g_scr[1])) > GROWTH_LIMIT)
    def _():
        meta_init()
        pipeline(safe_first, safe_step, nk - 2, safe_last, 2)

    o_t = [acc_scr[hh][:V_HEAD, :] / acc_scr[hh][V_HEAD:V_HEAD + 1, :] for hh in range(2)]
    o_ref[...] = jnp.concatenate(o_t, axis=0).T.astype(BF16)


def _attention(q, k, vt, km, vtm, bias, tq):
    b, s, _ = q.shape
    nk = k.shape[0] // b
    assert nk >= 2 and nk % 2 == 0, "the key-chunk pipeline is unrolled by two"
    tk = k.shape[1]
    hp = MLA_HEADS // 2
    w2 = 2 * LANES
    return pl.pallas_call(
        functools.partial(_attn_kernel, nk=nk),
        grid=(b, hp, s // tq),
        in_specs=[
            pl.BlockSpec((None, tq, w2), lambda bi, h, i: (bi, i, h)),
            pl.BlockSpec((nk, tk, w2), lambda bi, h, i: (bi, 0, h)),
            pl.BlockSpec((nk, 2 * VROWS, tk), lambda bi, h, i: (bi, h, 0)),
            pl.BlockSpec((MCHUNK, w2), lambda bi, h, i: (0, h)),
            pl.BlockSpec((2 * VROWS, MCHUNK), lambda bi, h, i: (h, 0)),
            pl.BlockSpec((MCHUNK, 1), lambda bi, h, i: (0, 0)),
        ],
        out_specs=pl.BlockSpec((None, tq, LANES), lambda bi, h, i: (bi, i, h)),
        out_shape=jax.ShapeDtypeStruct((b, s, hp * LANES), BF16),
        scratch_shapes=[
            pltpu.VMEM((2, 1, tq), F32), pltpu.VMEM((2, VROWS, tq), F32),
            pltpu.VMEM((2, 1, tq), F32),
            pltpu.VMEM((2, 2, 1, tq), F32), pltpu.VMEM((2, 2, 1, tq), F32),
            pltpu.VMEM((2, 2, tk, tq), F32), pltpu.VMEM((2, 2, tk, tq), BF16),
        ],
        compiler_params=pltpu.CompilerParams(
            dimension_semantics=("parallel", "parallel", "arbitrary"), vmem_limit_bytes=VMEM_LIMIT),
        name="mla_attention",
    )(q, k, vt, km, vtm, bias)


def _conv_silu(cur, prev_row, next_row, cw, cb):
    n = cur.shape[0]
    c0, c1, c2 = cw[0:1, :], cw[1:2, :], cw[2:3, :]
    y = c0 * pltpu.roll(cur, 1, axis=0) + c1 * cur + c2 * pltpu.roll(cur, n - 1, axis=0) + cb
    r8 = lax.broadcasted_iota(jnp.int32, (8, cur.shape[1]), 0)
    top = y[0:8, :] + jnp.where(r8 == 0, c0 * (prev_row - cur[n - 1:n, :]), 0.0)
    bot = y[n - 8:n, :] + jnp.where(r8 == 7, c2 * (next_row - cur[0:1, :]), 0.0)
    y = jnp.concatenate([top, y[8:n - 8, :], bot], axis=0)
    return y * _sigmoid(y)


def _log_sigmoid(x):
    return jnp.minimum(x, 0.0) - jnp.log(1.0 + jnp.exp(-jnp.abs(x)))


def _tri(n, upper):
    r = lax.broadcasted_iota(jnp.int32, (n, n), 0)
    c = lax.broadcasted_iota(jnp.int32, (n, n), 1)
    return (c >= r) if upper else (c <= r)


FF_CHUNK = 1024
VROWS = LANES
AUG = 16
CROWS = M_HEAD_DIM + AUG


def _v_aug_t(vt_h):
    r = lax.broadcasted_iota(jnp.int32, (AUG, vt_h.shape[1]), 0)
    return jnp.concatenate([vt_h, jnp.where(r == 0, 1.0, 0.0).astype(vt_h.dtype)], axis=0)


def _split3(a):
    hi = a.astype(BF16)
    r = a - hi.astype(F32)
    mid = r.astype(BF16)
    return hi, mid, (r - mid.astype(F32)).astype(BF16)


def _gate_rows(gi8, gf8, valid, tri_t, last, m_prev8):
    a8 = _log_sigmoid(gf8)
    b8 = gi8
    if valid is not None:
        a8 = jnp.where(valid, a8, 0.0)
        b8 = jnp.where(valid, b8, NEG)
    f8 = sum(_dot(part, tri_t) for part in _split3(a8))
    f_tot = f8[:, last:last + 1]
    g8 = f_tot - f8 + b8
    m_new8 = jnp.maximum(f_tot + m_prev8, jnp.max(g8, axis=1, keepdims=True))
    return f8, jnp.exp(g8 - m_new8), jnp.exp(f_tot + m_prev8 - m_new8), m_new8


def _state_update(k_h, vaug_t, w_row, decay, ct_ref, h):
    vw = (vaug_t.astype(F32) * w_row).astype(BF16)
    ct_ref[h] = decay * ct_ref[h] + _dot(vw, k_h)


def _mlstm_chunk(q, k, vt_ref, gc, gr, ct_ref, m_ref, reverse):
    n = q.shape[0]
    r0 = M_HEADS if reverse else 0
    mask_t = _tri(n, not reverse)
    tri_t = mask_t.astype(BF16)
    tri_c = _tri(n, reverse).astype(BF16)
    last = 0 if reverse else n - 1
    m_prev8 = m_ref[:, 0:1]
    f8, w8, decay8, m_new8 = _gate_rows(gr[0:8, :], gr[8:16, :], None, tri_t, last, m_prev8)
    inter8 = f8 + m_prev8
    c_cols = sum(_dot(tri_c, part) for part in _split3(_log_sigmoid(gc[:, LANES:]))) - gc[:, :LANES]
    outs = []
    for h in range(M_HEADS):
        r = r0 + h
        sl = slice(h * M_HEAD_DIM, (h + 1) * M_HEAD_DIM)
        q_h = q[:, sl].astype(BF16)
        k_h = k[:, sl].astype(BF16)
        vaug_t = _v_aug_t(vt_ref[sl, :])
        inter = inter8[r:r + 1, :]
        log_d = jnp.where(mask_t, f8[r:r + 1, :] - c_cols[:, r:r + 1], NEG)
        m_j = jnp.maximum(inter, jnp.max(log_d, axis=0, keepdims=True))
        s_t = (_dot_nt(k_h, q_h) * jnp.exp(log_d - m_j)).astype(BF16)
        h_t = _dot_nt(ct_ref[h].astype(BF16), q_h) * jnp.exp(inter - m_j) + _dot(vaug_t, s_t)
        den = h_t[M_HEAD_DIM:M_HEAD_DIM + 1, :]
        out_t = h_t[:M_HEAD_DIM, :] / jnp.maximum(jnp.abs(den), jnp.exp(-m_j))
        outs.append(out_t.T)
        _state_update(k_h, vaug_t, w8[r:r + 1, :], decay8[r:r + 1, :], ct_ref, h)
    m_ref[...] = jnp.broadcast_to(m_new8, m_ref.shape)
    return jnp.concatenate(outs, axis=1)


def _mlstm_kernel(qk_f, pv_f, nx_f, vt_f, gc_f, gr_f, qk_b, pv_b, nx_b, vt_b, gc_b, gr_b,
                  qk_m, vt_m, gr_m, cw_ref, cb_ref, hf_out, hb_out, cf_scr, mf_scr, cb_scr, mb_scr):
    j = pl.program_id(1)
    nc = pl.num_programs(1)
    cw = cw_ref[...]
    cb = cb_ref[...]
    scale = M_HEAD_DIM ** -0.5
    meta_last = qk_m[MCHUNK - 1:MCHUNK, :].astype(F32)

    @pl.when(j == 0)
    def _():
        cf_scr[...] = jnp.zeros_like(cf_scr)
        cb_scr[...] = jnp.zeros_like(cb_scr)
        mb_scr[...] = jnp.zeros_like(mb_scr)
        valid_col = lax.broadcasted_iota(jnp.int32, (MCHUNK, 1), 0) >= META_PAD
        valid_row = lax.broadcasted_iota(jnp.int32, (1, MCHUNK), 1) >= META_PAD
        cur = jnp.where(valid_col, qk_m[...].astype(F32), 0.0)
        first_x = qk_f[0:1, :].astype(F32)
        k_all = _conv_silu(cur, jnp.zeros_like(first_x), first_x, cw, cb)[:, M_WIDTH:] * scale
        _, w8, decay8, m_new8 = _gate_rows(gr_m[0:8, :], gr_m[8:16, :], valid_row, _tri(MCHUNK, True).astype(BF16),
                                           MCHUNK - 1, jnp.zeros((8, 1), F32))
        for h in range(M_HEADS):
            sl = slice(h * M_HEAD_DIM, (h + 1) * M_HEAD_DIM)
            _state_update(k_all[:, sl].astype(BF16), _v_aug_t(vt_m[sl, :]), w8[h:h + 1, :], decay8[h:h + 1, :],
                          cf_scr, h)
        mf_scr[...] = jnp.broadcast_to(m_new8, mf_scr.shape)

    def run(qk_ref, pv_ref, nx_ref, vt_ref, gc_ref, gr_ref, c, c_scr, m_scr, reverse, out_ref):
        prev_row = jnp.where(c == 0, meta_last, pv_ref[HALO - 1:HALO, :].astype(F32))
        next_row = jnp.where(c == nc - 1, 0.0, nx_ref[0:1, :].astype(F32))
        qk = _conv_silu(qk_ref[...].astype(F32), prev_row, next_row, cw, cb)
        h = _mlstm_chunk(qk[:, :M_WIDTH], qk[:, M_WIDTH:] * scale, vt_ref, gc_ref[...], gr_ref[...],
                         c_scr, m_scr, reverse)
        out_ref[...] = h.astype(out_ref.dtype)

    run(qk_f, pv_f, nx_f, vt_f, gc_f, gr_f, j, cf_scr, mf_scr, False, hf_out)
    run(qk_b, pv_b, nx_b, vt_b, gc_b, gr_b, nc - 1 - j, cb_scr, mb_scr, True, hb_out)


def _mlstm(mqk, mvt, gcol, grow, mqk_m, mvt_m, grow_m, conv_w, conv_b):
    b, s, _ = mqk.shape
    nc = s // LCHUNK
    hpc = LCHUNK // HALO
    nhb = s // HALO
    cps = mvt.shape[2] // LCHUNK
    spb = mvt.shape[0] // b

    def specs(chunk_of):
        return [
            pl.BlockSpec((None, LCHUNK, 2 * M_WIDTH), lambda bi, j: (bi, chunk_of(j), 0)),
            pl.BlockSpec((None, HALO, 2 * M_WIDTH), lambda bi, j: (bi, jnp.maximum(chunk_of(j) * hpc - 1, 0), 0)),
            pl.BlockSpec((None, HALO, 2 * M_WIDTH),
                         lambda bi, j: (bi, jnp.minimum((chunk_of(j) + 1) * hpc, nhb - 1), 0)),
            pl.BlockSpec((None, M_WIDTH, LCHUNK), lambda bi, j: (bi * spb + chunk_of(j) // cps, 0, chunk_of(j) % cps)),
            pl.BlockSpec((None, LCHUNK, 2 * LANES), lambda bi, j: (bi, chunk_of(j), 0)),
            pl.BlockSpec((4 * M_HEADS, LCHUNK), lambda bi, j: (0, bi * nc + chunk_of(j))),
        ]

    fwd = lambda j: j
    bwd = lambda j: nc - 1 - j
    operands = [mqk, mqk, mqk, mvt, gcol, grow]
    out_spec = lambda chunk_of: pl.BlockSpec((None, LCHUNK, M_WIDTH), lambda bi, j: (bi, chunk_of(j), 0))
    state = [pltpu.VMEM((M_HEADS, CROWS, M_HEAD_DIM), F32), pltpu.VMEM((2 * M_HEADS, LANES), F32)]
    return pl.pallas_call(
        _mlstm_kernel,
        grid=(b, nc),
        in_specs=specs(fwd) + specs(bwd) + [_const_spec(a.shape) for a in (mqk_m, mvt_m, grow_m, conv_w, conv_b)],
        out_specs=[out_spec(fwd), out_spec(bwd)],
        out_shape=[jax.ShapeDtypeStruct((b, s, M_WIDTH), BF16)] * 2,
        scratch_shapes=state + state,
        compiler_params=pltpu.CompilerParams(
            dimension_semantics=("parallel", "arbitrary"), vmem_limit_bytes=VMEM_LIMIT),
        name="mlstm",
    )(*operands, *operands, mqk_m, mvt_m, grow_m, conv_w, conv_b)


def _out_kernel(x_ref, a_ref, hf_ref, hb_ref, mo_ref, sga_ref, sgb_ref,
                mg_ref, wpa_ref, wpb_ref, wo_ref, n2_ref, wg_ref, wu_ref, wd_ref, fn_ref, o_ref):
    h = (hf_ref[...].astype(F32) + hb_ref[...].astype(F32)) * _sigmoid(mo_ref[...].astype(F32))
    mg = mg_ref[...]
    parts = []
    for hd in range(M_HEADS):
        sl = slice(hd * M_HEAD_DIM, (hd + 1) * M_HEAD_DIM)
        parts.append(_rms(h[:, sl], mg[:, sl]).astype(BF16))
    m_out = jnp.concatenate(parts, axis=1)
    merged = (sga_ref[...].astype(F32) * _dot(a_ref[...], wpa_ref[...])
              + sgb_ref[...].astype(F32) * _dot(m_out, wpb_ref[...]))
    y = x_ref[...] + _dot(merged.astype(BF16), wo_ref[...])

    xn = _rms(y, n2_ref[...]).astype(BF16)
    d_ff = wg_ref.shape[1]
    for c0 in range(0, d_ff, FF_CHUNK):
        c1 = min(c0 + FF_CHUNK, d_ff)
        g = _dot(xn, wg_ref[:, c0:c1])
        u = _dot(xn, wu_ref[:, c0:c1])
        y = y + _dot((g * _sigmoid(g) * u).astype(BF16), wd_ref[c0:c1, :])
    o_ref[...] = _rms(y, fn_ref[...])


def _out_proj(x2, a2, hf2, hb2, mo2, sga2, sgb2, wts, tb):
    n, d = x2.shape
    row = lambda w: pl.BlockSpec((tb, w), lambda i: (i, 0))
    const = lambda w: pl.BlockSpec(w.shape, lambda i: (0,) * w.ndim, pipeline_mode=pl.Buffered(1))
    return pl.pallas_call(
        _out_kernel,
        grid=(n // tb,),
        in_specs=[row(d), row(a2.shape[1]), row(M_WIDTH), row(M_WIDTH), row(M_WIDTH), row(d), row(d)]
        + [const(w) for w in wts],
        out_specs=row(d),
        out_shape=jax.ShapeDtypeStruct((n, d), F32),
        compiler_params=pltpu.CompilerParams(dimension_semantics=("parallel",), vmem_limit_bytes=VMEM_LIMIT),
        name="out_proj",
    )(x2, a2, hf2, hb2, mo2, sga2, sgb2, *wts)


def _rope_tables(pos):
    half = QK_ROPE // 2
    n = pos.shape[0]
    freqs = ROPE_THETA ** (-jnp.arange(half, dtype=F32) / half)
    per_row = LANES // half
    ang = jnp.repeat(pos.astype(F32).reshape(n // per_row, per_row), half, axis=1) * jnp.tile(freqs, per_row)[None, :]
    c, s = jnp.cos(ang).reshape(n, half), jnp.sin(ang).reshape(n, half)
    lead = jnp.zeros((n, QK_NOPE), F32)
    tail = jnp.zeros((n, LANES - QK_DIM), F32)
    return jnp.concatenate([lead, c, c, tail], axis=1), jnp.concatenate([lead, s, s, tail], axis=1)


def _rot(w):
    half = QK_ROPE // 2
    return jnp.concatenate([-w[..., half:], w[..., :half]], axis=-1)


def _pad_cols(w, left, total):
    return jnp.pad(w, [(0, 0)] * (w.ndim - 1) + [(left, total - left - w.shape[-1])])


def _prep_weights(norm1_g, w_in, b_in, q_norm_g, kv_norm_g, w_uq, w_ukv):
    d = w_in.shape[0]
    o_kr = Q_LORA + KV_LORA
    o_m = o_kr + QK_ROPE
    o_g = o_m + 4 * M_WIDTH
    o_ga = o_g + 4 * M_HEADS

    def seg1(w):
        kr = w[..., o_kr:o_m]
        lead = jnp.zeros(w.shape[:-1] + (QK_NOPE,), w.dtype)
        return jnp.concatenate([w[..., :o_kr], lead, kr, kr, lead, _rot(kr), _rot(kr)], axis=-1)

    def gates8(w):
        gates = w[..., o_g:o_ga].reshape(w.shape[:-1] + (4, M_HEADS))
        return (jnp.concatenate([gates[..., 0, :], gates[..., 2, :]], axis=-1),
                jnp.concatenate([gates[..., 1, :], gates[..., 3, :]], axis=-1))

    def segm(w):
        g_in, g_fg = gates8(w)
        return jnp.concatenate([w[..., o_m:o_m + 2 * M_WIDTH], w[..., o_m + 3 * M_WIDTH:o_g],
                                _pad_cols(g_in, 0, LANES), _pad_cols(g_fg, 0, LANES)], axis=-1)

    def segv(w):
        return jnp.concatenate([w[..., o_m + 2 * M_WIDTH:o_m + 3 * M_WIDTH], *gates8(w)], axis=-1)

    b2 = b_in[None, :]
    w1, b1 = seg1(w_in).astype(BF16), seg1(b2)
    wm, bm = segm(w_in).astype(BF16), segm(b2)
    wg, bg = w_in[:, o_ga:].astype(BF16), b2[:, o_ga:]

    uq = w_uq.reshape(Q_LORA, MLA_HEADS, QK_DIM)
    wuq = jnp.concatenate([uq, _rot(uq[..., QK_NOPE:])], axis=-1).reshape(Q_LORA, MLA_HEADS * LANES).astype(BF16)
    ukv = w_ukv.reshape(KV_LORA, MLA_HEADS, QK_NOPE + V_HEAD)
    wuk = _pad_cols(ukv[..., :QK_NOPE], 0, LANES).reshape(KV_LORA, MLA_HEADS * LANES).astype(BF16)
    wuv = _pad_cols(ukv[..., QK_NOPE:], 0, VROWS).reshape(KV_LORA, MLA_HEADS * VROWS).T.astype(BF16)
    vone = jnp.tile((jnp.arange(VROWS) == V_HEAD).astype(F32), MLA_HEADS)[:, None]
    wmv, bmv = segv(w_in).T.astype(BF16), segv(b2).T
    return (norm1_g[None, :], w1, b1, wm, bm, wg, bg, q_norm_g[None, :], kv_norm_g[None, :], wuq, wuk, wuv, vone,
            wmv, bmv)


def _pick(n, pref):
    return pref if n % pref == 0 else n


def _attn_tq(s):
    return _pick(s, max(512, min(1024, ATTN_STEP_SCORES // s)))


def _trunk(x, rope, meta_parts, in_wts, conv_w, conv_b, merge_wts, ffn_wts):
    b, s, d = x.shape
    n = b * s
    tb = _pick(s, 512)
    k_m, vt_m, mqk_m, mvt_m, gr_m = meta_parts

    x2 = x.reshape(n, d)
    q, k, vt, mqk, mvt, mo, gt, sga, sgb, gr = _in_proj(x2, rope, in_wts, tb, s)

    r3 = lambda a: a.reshape(b, s, a.shape[-1])
    bias = jnp.where(jnp.arange(MCHUNK) >= META_PAD, 0.0, NEG).astype(F32)[:, None]
    a_out = _attention(r3(q), k.reshape(n // tb, tb, k.shape[-1]), vt, k_m, vt_m, bias, _attn_tq(s))

    h_f, h_b = _mlstm(r3(mqk), mvt, r3(gt), gr, mqk_m, mvt_m, gr_m, conv_w, conv_b)

    y = _out_proj(x2, a_out.reshape(n, -1), h_f.reshape(n, -1), h_b.reshape(n, -1), mo, sga, sgb,
                  merge_wts + ffn_wts, tb)
    return y.reshape(b, s, d)


def kernel(x_prompt, x_sample, meta_tokens, norm1_g, w_in, b_in, conv_w, conv_b, q_norm_g, kv_norm_g, w_uq, w_ukv, m_norm_g, w_pa, w_pb, w_o, norm2_g, w_ffn_gate, w_ffn_up, w_ffn_down, final_norm_g):
    assert w_in.shape[0] == 1, "single-layer trunk"
    d = x_prompt.shape[-1]
    in_wts = _prep_weights(norm1_g[0], w_in[0], b_in[0], q_norm_g[0], kv_norm_g[0], w_uq[0], w_ukv[0])
    merge_wts = (m_norm_g[0][None, :], w_pa[0].astype(BF16), w_pb[0].astype(BF16), w_o[0].astype(BF16))
    ffn_wts = (norm2_g[0][None, :], w_ffn_gate[0].astype(BF16), w_ffn_up[0].astype(BF16),
               w_ffn_down[0].astype(BF16), final_norm_g[None, :])
    cw, cb = conv_w[0], conv_b[0][None, :]

    hm = jnp.concatenate([jnp.zeros((META_PAD, d), x_prompt.dtype), meta_tokens.astype(x_prompt.dtype)], axis=0)
    rope_m = _rope_tables(jnp.arange(MCHUNK) - META_PAD)
    _, k_m, vt_m, mqk_m, mvt_m, _, _, _, _, gr_m = _in_proj(hm, rope_m, in_wts, MCHUNK, MCHUNK)
    meta_parts = (k_m, vt_m[0], mqk_m, mvt_m[0], gr_m)

    rope = _rope_tables(N_META + jnp.arange(max(x_prompt.shape[1], x_sample.shape[1])))
    y_prompt = _trunk(x_prompt, rope, meta_parts, in_wts, cw, cb, merge_wts, ffn_wts)
    y_sample = _trunk(x_sample, rope, meta_parts, in_wts, cw, cb, merge_wts, ffn_wts)
    return (y_prompt, y_sample)
```

```python
import functools

import jax
import jax.numpy as jnp
from jax import lax
from jax.experimental import pallas as pl
from jax.experimental.pallas import tpu as pltpu

F32 = jnp.float32
BF16 = jnp.bfloat16

EPS = 1e-6
N_META = 16
ROPE_THETA = 10000.0
MLA_HEADS = 8
QK_NOPE = 64
QK_ROPE = 32
V_HEAD = 64
QK_DIM = QK_NOPE + QK_ROPE
Q_LORA = 256
KV_LORA = 128
M_HEADS = 4
M_HEAD_DIM = 128
M_WIDTH = M_HEADS * M_HEAD_DIM
LANES = 128
MCHUNK = 128
LCHUNK = 256
META_PAD = MCHUNK - N_META
HALO = 16
NEG = -1e30
LOG2E = 1.4426950408889634
GROWTH_LIMIT = 64.0
ATTN_STEP_SCORES = 512 * 16384
FAST_UNROLL = 16
VMEM_LIMIT = 56 * 1024 * 1024


def _dot(a, b):
    return jnp.dot(a, b, preferred_element_type=F32)


def _dot_nt(a, b):
    return lax.dot_general(a, b, (((1,), (1,)), ((), ())), preferred_element_type=F32)


def _dot_tn(a, b):
    return lax.dot_general(a, b, (((0,), (0,)), ((), ())), preferred_element_type=F32)


def _sigmoid(x):
    return 1.0 / (1.0 + jnp.exp(-x))


def _rms(x, g):
    return x * lax.rsqrt(jnp.mean(x * x, axis=-1, keepdims=True) + EPS) * g


def _const_spec(shape):
    return pl.BlockSpec(shape, lambda *_: (0,) * len(shape))


def _in_proj_kernel(x_ref, cos_ref, sin_ref, n1_ref, w1_ref, b1_ref, wm_ref, bm_ref, wg_ref, bg_ref,
                    qn_ref, kvn_ref, wuq_ref, wuk_ref, wuv_ref, vone_ref, wmv_ref, bmv_ref,
                    q_out, k_out, v_out, mqk_out, mv_out, mo_out, gt_out, sga_out, sgb_out, gr_out):
    xn = _rms(x_ref[...], n1_ref[...]).astype(BF16)
    cos, sin = cos_ref[...], sin_ref[...]
    cos_hi, sin_hi = pltpu.roll(cos, QK_ROPE, axis=1), pltpu.roll(sin, QK_ROPE, axis=1)
    lane = lax.broadcasted_iota(jnp.int32, cos.shape, 1)
    tq = jnp.where(lane < QK_NOPE, 1.0, cos + sin_hi)

    p1 = _dot(xn, w1_ref[...]) + b1_ref[...]
    c_q = p1[:, :Q_LORA]
    c_kv = p1[:, Q_LORA:Q_LORA + KV_LORA]
    kr_plain = p1[:, Q_LORA + KV_LORA:Q_LORA + KV_LORA + LANES]
    kr_rot = p1[:, Q_LORA + KV_LORA + LANES:]
    k_rope = kr_plain * (cos + cos_hi) + kr_rot * (sin + sin_hi)

    cqn = (_rms(c_q, qn_ref[...]) * (QK_DIM ** -0.5 * LOG2E)).astype(BF16)
    qa = _dot(cqn, wuq_ref[...])
    ckvn = _rms(c_kv, kvn_ref[...]).astype(BF16)
    ka = _dot(ckvn, wuk_ref[...])
    for h in range(MLA_HEADS):
        sl = slice(h * LANES, (h + 1) * LANES)
        q_out[:, sl] = (qa[:, sl] * tq).astype(BF16)
        k_out[:, sl] = (ka[:, sl] + k_rope).astype(BF16)
    v_out[...] = (_dot_nt(wuv_ref[...], ckvn) + vone_ref[...]).astype(BF16)

    pm = _dot(xn, wm_ref[...]) + bm_ref[...]
    mqk_out[...] = pm[:, :2 * M_WIDTH].astype(BF16)
    mo_out[...] = pm[:, 2 * M_WIDTH:3 * M_WIDTH].astype(BF16)
    gt_out[...] = pm[:, 3 * M_WIDTH:]
    mt = _dot_nt(wmv_ref[...], xn) + bmv_ref[...]
    mv_out[...] = mt[:M_WIDTH, :].astype(BF16)
    gr_out[...] = mt[M_WIDTH:, :]

    pg = _dot(xn, wg_ref[...]) + bg_ref[...]
    d = sga_out.shape[-1]
    sga_out[...] = _sigmoid(pg[:, :d]).astype(BF16)
    sgb_out[...] = _sigmoid(pg[:, d:]).astype(BF16)


def _in_proj(x2, rope_tabs, wts, tb, seq):
    n, d = x2.shape
    rope = pl.BlockSpec((tb, LANES), lambda i: (i % (seq // tb), 0))
    widths = (MLA_HEADS * LANES, MLA_HEADS * LANES, MLA_HEADS * VROWS, 2 * M_WIDTH, M_WIDTH, M_WIDTH, 2 * LANES, d, d)
    dtypes = (BF16, BF16, BF16, BF16, BF16, BF16, F32, BF16, BF16)
    row = lambda w: pl.BlockSpec((tb, w), lambda i: (i, 0))
    out_specs = [row(w) for w in widths]
    out_shape = [jax.ShapeDtypeStruct((n, w), t) for w, t in zip(widths, dtypes)]
    for o in (2, 4):
        out_specs[o] = pl.BlockSpec((None, widths[o], tb), lambda i: (i, 0, 0))
        out_shape[o] = jax.ShapeDtypeStruct((n // tb, widths[o], tb), BF16)
    out_specs.append(pl.BlockSpec((4 * M_HEADS, tb), lambda i: (0, i)))
    out_shape.append(jax.ShapeDtypeStruct((4 * M_HEADS, n), F32))
    return pl.pallas_call(
        _in_proj_kernel,
        grid=(n // tb,),
        in_specs=[row(d)] + [rope] * len(rope_tabs) + [_const_spec(w.shape) for w in wts],
        out_specs=out_specs,
        out_shape=out_shape,
        compiler_params=pltpu.CompilerParams(dimension_semantics=("parallel",), vmem_limit_bytes=VMEM_LIMIT),
        name="in_proj",
    )(x2, *rope_tabs, *wts)


def _attn_kernel(q_ref, k_ref, vt_ref, km_ref, vtm_ref, bias_ref, o_ref,
                 m_scr, acc_scr, g_scr, mx_scr, al_scr, s_scr, p_scr, *, nk):
    hsl = [slice(0, LANES), slice(LANES, 2 * LANES)]
    vsl = [slice(0, VROWS), slice(VROWS, 2 * VROWS)]
    qs = [q_ref[:, sl] for sl in hsl]

    def meta_init():
        for hh, sl in enumerate(hsl):
            s = _dot_nt(km_ref[:, sl], qs[hh]) + bias_ref[...]
            m = jnp.max(s, axis=0, keepdims=True)
            p = jnp.exp2(s - m).astype(BF16)
            m_scr[hh] = m
            acc_scr[hh] = _dot(vtm_ref[vsl[hh], :], p)

    def pipeline(first, step, n_steps, last, unroll):
        first()

        def body(t, carry):
            for u in range(unroll):
                step(unroll * t + u, u % 2)
            return carry

        n_loop = n_steps // unroll
        lax.fori_loop(0, n_loop, body, 0)
        for i in range(n_loop * unroll, n_steps):
            step(i, i % 2)
        last()

    def scores_exp(c, slot):
        for hh, sl in enumerate(hsl):
            s = _dot_nt(k_ref[c, :, sl], qs[hh])
            ref = m_scr[hh]
            p_scr[slot, hh] = jnp.exp2(s - ref).astype(BF16)
            m_new = jnp.maximum(ref, jnp.max(s, axis=0, keepdims=True))
            al_scr[slot, hh] = jnp.exp2(ref - m_new)
            g_scr[hh] = jnp.maximum(g_scr[hh], m_new - ref)
            m_scr[hh] = m_new

    def values_rebase(c, slot):
        for hh, sl in enumerate(hsl):
            acc_scr[hh] = (acc_scr[hh] + _dot(vt_ref[c, vsl[hh], :], p_scr[slot, hh])) * al_scr[slot, hh]

    def fast_step(i, par):
        scores_exp(i + 1, 1 - par)
        values_rebase(i, par)

    meta_init()
    g_scr[...] = jnp.zeros_like(g_scr)
    pipeline(lambda: scores_exp(0, 0), fast_step, nk - 1, lambda: values_rebase(nk - 1, (nk - 1) % 2), FAST_UNROLL)

    def scores(c, slot):
        for hh, sl in enumerate(hsl):
            s = _dot_nt(k_ref[c, :, sl], qs[hh])
            mx_scr[slot, hh] = jnp.max(s, axis=0, keepdims=True)
            s_scr[slot, hh] = s

    def softmax(slot):
        for hh in range(2):
            m_prev = m_scr[hh]
            m_new = jnp.maximum(m_prev, mx_scr[slot, hh])
            al_scr[slot, hh] = jnp.exp2(m_prev - m_new)
            m_scr[hh] = m_new
            p_scr[slot, hh] = jnp.exp2(s_scr[slot, hh] - m_new).astype(BF16)

    def values(c, slot):
        for hh, sl in enumerate(hsl):
            acc_scr[hh] = al_scr[slot, hh] * acc_scr[hh] + _dot(vt_ref[c, vsl[hh], :], p_scr[slot, hh])

    def safe_first():
        scores(0, 0)
        scores(1, 1)
        softmax(0)

    def safe_step(i, par):
        scores(i + 2, par)
        softmax(1 - par)
        values(i, par)

    def safe_last():
        softmax(1)
        values(nk - 2, 0)
        values(nk - 1, 1)

    @pl.when(jnp.max(jnp.maximum(g_scr[0], g_scr[1])) > GROWTH_LIMIT)
    def _():
        meta_init()
        pipeline(safe_first, safe_step, nk - 2, safe_last, 2)

    o_t = [acc_scr[hh][:V_HEAD, :] / acc_scr[hh][V_HEAD:V_HEAD + 1, :] for hh in range(2)]
    o_ref[...] = jnp.concatenate(o_t, axis=0).T.astype(BF16)


def _attention(q, k, vt, km, vtm, bias, tq):
    b, s, _ = q.shape
    nk = k.shape[0] // b
    assert nk >= 2 and nk % 2 == 0, "the key-chunk pipeline is unrolled by two"
    tk = k.shape[1]
    hp = MLA_HEADS // 2
    w2 = 2 * LANES
    return pl.pallas_call(
        functools.partial(_attn_kernel, nk=nk),
        grid=(b, hp, s // tq),
        in_specs=[
            pl.BlockSpec((None, tq, w2), lambda bi, h, i: (bi, i, h)),
            pl.BlockSpec((nk, tk, w2), lambda bi, h, i: (bi, 0, h)),
            pl.BlockSpec((nk, 2 * VROWS, tk), lambda bi, h, i: (bi, h, 0)),
            pl.BlockSpec((MCHUNK, w2), lambda bi, h, i: (0, h)),
            pl.BlockSpec((2 * VROWS, MCHUNK), lambda bi, h, i: (h, 0)),
            pl.BlockSpec((MCHUNK, 1), lambda bi, h, i: (0, 0)),
        ],
        out_specs=pl.BlockSpec((None, tq, LANES), lambda bi, h, i: (bi, i, h)),
        out_shape=jax.ShapeDtypeStruct((b, s, hp * LANES), BF16),
        scratch_shapes=[
            pltpu.VMEM((2, 1, tq), F32), pltpu.VMEM((2, VROWS, tq), F32),
            pltpu.VMEM((2, 1, tq), F32),
            pltpu.VMEM((2, 2, 1, tq), F32), pltpu.VMEM((2, 2, 1, tq), F32),
            pltpu.VMEM((2, 2, tk, tq), F32), pltpu.VMEM((2, 2, tk, tq), BF16),
        ],
        compiler_params=pltpu.CompilerParams(
            dimension_semantics=("parallel", "parallel", "arbitrary"), vmem_limit_bytes=VMEM_LIMIT),
        name="mla_attention",
    )(q, k, vt, km, vtm, bias)


def _shift_matrix(n):
    r = lax.broadcasted_iota(jnp.int32, (2 * n, n), 0)
    c = lax.broadcasted_iota(jnp.int32, (2 * n, n), 1)
    return jnp.where(c == jnp.where(r < n, r - 1, r - n + 1), 1.0, 0.0).astype(BF16)


def _conv_silu(cur, prev_row, next_row, cw, cb, shift):
    n = cur.shape[0]
    c0, c1, c2 = cw[0:1, :], cw[1:2, :], cw[2:3, :]
    nb = _dot(shift, cur.astype(BF16))
    y = c0 * nb[:n, :] + c1 * cur + c2 * nb[n:, :] + cb
    r8 = lax.broadcasted_iota(jnp.int32, (8, cur.shape[1]), 0)
    top = y[0:8, :] + jnp.where(r8 == 0, c0 * prev_row, 0.0)
    bot = y[n - 8:n, :] + jnp.where(r8 == 7, c2 * next_row, 0.0)
    y = jnp.concatenate([top, y[8:n - 8, :], bot], axis=0)
    return y * _sigmoid(y)


def _log_sigmoid(x):
    return jnp.minimum(x, 0.0) - jnp.log(1.0 + jnp.exp(-jnp.abs(x)))


def _tri(n, upper):
    r = lax.broadcasted_iota(jnp.int32, (n, n), 0)
    c = lax.broadcasted_iota(jnp.int32, (n, n), 1)
    return (c >= r) if upper else (c <= r)


FF_CHUNK = 1024
VROWS = LANES
AUG = 16
CROWS = M_HEAD_DIM + AUG


def _v_aug_t(vt_h):
    r = lax.broadcasted_iota(jnp.int32, (AUG, vt_h.shape[1]), 0)
    return jnp.concatenate([vt_h, jnp.where(r == 0, 1.0, 0.0).astype(vt_h.dtype)], axis=0)


def _split3(a):
    hi = a.astype(BF16)
    r = a - hi.astype(F32)
    mid = r.astype(BF16)
    return hi, mid, (r - mid.astype(F32)).astype(BF16)


def _gate_rows(gi8, gf8, valid, tri_t, last, m_prev8):
    a8 = _log_sigmoid(gf8)
    b8 = gi8
    if valid is not None:
        a8 = jnp.where(valid, a8, 0.0)
        b8 = jnp.where(valid, b8, NEG)
    f8 = sum(_dot(part, tri_t) for part in _split3(a8))
    f_tot = f8[:, last:last + 1]
    g8 = f_tot - f8 + b8
    m_new8 = jnp.maximum(f_tot + m_prev8, jnp.max(g8, axis=1, keepdims=True))
    return f8, jnp.exp(g8 - m_new8), jnp.exp(f_tot + m_prev8 - m_new8), m_new8


def _state_update(k_h, vaug_t, w_row, decay, ct_ref, h):
    vw = (vaug_t.astype(F32) * w_row).astype(BF16)
    ct_ref[h] = decay * ct_ref[h] + _dot(vw, k_h)


def _mlstm_chunk(q, k, vt_ref, gc, gr, ct_ref, m_ref, reverse):
    n = q.shape[0]
    r0 = M_HEADS if reverse else 0
    mask_t = _tri(n, not reverse)
    tri_t = mask_t.astype(BF16)
    tri_c = _tri(n, reverse).astype(BF16)
    last = 0 if reverse else n - 1
    m_prev8 = m_ref[:, 0:1]
    f8, w8, decay8, m_new8 = _gate_rows(gr[0:8, :], gr[8:16, :], None, tri_t, last, m_prev8)
    inter8 = f8 + m_prev8
    c_cols = sum(_dot(tri_c, part) for part in _split3(_log_sigmoid(gc[:, LANES:]))) - gc[:, :LANES]
    outs = []
    for h in range(M_HEADS):
        r = r0 + h
        sl = slice(h * M_HEAD_DIM, (h + 1) * M_HEAD_DIM)
        q_h = q[:, sl].astype(BF16)
        k_h = k[:, sl].astype(BF16)
        vaug_t = _v_aug_t(vt_ref[sl, :])
        inter = inter8[r:r + 1, :]
        log_d = jnp.where(mask_t, f8[r:r + 1, :] - c_cols[:, r:r + 1], NEG)
        m_j = jnp.maximum(inter, jnp.max(log_d, axis=0, keepdims=True))
        s_t = (_dot_nt(k_h, q_h) * jnp.exp(log_d - m_j)).astype(BF16)
        h_t = _dot_nt(ct_ref[h].astype(BF16), q_h) * jnp.exp(inter - m_j) + _dot(vaug_t, s_t)
        den = h_t[M_HEAD_DIM:M_HEAD_DIM + 1, :]
        out_t = h_t[:M_HEAD_DIM, :] / jnp.maximum(jnp.abs(den), jnp.exp(-m_j))
        outs.append(out_t.T)
        _state_update(k_h, vaug_t, w8[r:r + 1, :], decay8[r:r + 1, :], ct_ref, h)
    m_ref[...] = jnp.broadcast_to(m_new8, m_ref.shape)
    return jnp.concatenate(outs, axis=1)


def _mlstm_kernel(qk_f, pv_f, nx_f, vt_f, gc_f, gr_f, qk_b, pv_b, nx_b, vt_b, gc_b, gr_b,
                  qk_m, vt_m, gr_m, cw_ref, cb_ref, hf_out, hb_out, cf_scr, mf_scr, cb_scr, mb_scr):
    j = pl.program_id(1)
    nc = pl.num_programs(1)
    cw = cw_ref[...]
    cb = cb_ref[...]
    scale = M_HEAD_DIM ** -0.5
    meta_last = qk_m[MCHUNK - 1:MCHUNK, :].astype(F32)

    @pl.when(j == 0)
    def _():
        cf_scr[...] = jnp.zeros_like(cf_scr)
        cb_scr[...] = jnp.zeros_like(cb_scr)
        mb_scr[...] = jnp.zeros_like(mb_scr)
        valid_col = lax.broadcasted_iota(jnp.int32, (MCHUNK, 1), 0) >= META_PAD
        valid_row = lax.broadcasted_iota(jnp.int32, (1, MCHUNK), 1) >= META_PAD
        cur = jnp.where(valid_col, qk_m[...].astype(F32), 0.0)
        first_x = qk_f[0:1, :].astype(F32)
        k_all = _conv_silu(cur, jnp.zeros_like(first_x), first_x, cw, cb, _shift_matrix(MCHUNK))[:, M_WIDTH:] * scale
        _, w8, decay8, m_new8 = _gate_rows(gr_m[0:8, :], gr_m[8:16, :], valid_row, _tri(MCHUNK, True).astype(BF16),
                                           MCHUNK - 1, jnp.zeros((8, 1), F32))
        for h in range(M_HEADS):
            sl = slice(h * M_HEAD_DIM, (h + 1) * M_HEAD_DIM)
            _state_update(k_all[:, sl].astype(BF16), _v_aug_t(vt_m[sl, :]), w8[h:h + 1, :], decay8[h:h + 1, :],
                          cf_scr, h)
        mf_scr[...] = jnp.broadcast_to(m_new8, mf_scr.shape)

    shift = _shift_matrix(qk_f.shape[0])

    def run(qk_ref, pv_ref, nx_ref, vt_ref, gc_ref, gr_ref, c, c_scr, m_scr, reverse, out_ref):
        prev_row = jnp.where(c == 0, meta_last, pv_ref[HALO - 1:HALO, :].astype(F32))
        next_row = jnp.where(c == nc - 1, 0.0, nx_ref[0:1, :].astype(F32))
        qk = _conv_silu(qk_ref[...].astype(F32), prev_row, next_row, cw, cb, shift)
        h = _mlstm_chunk(qk[:, :M_WIDTH], qk[:, M_WIDTH:] * scale, vt_ref, gc_ref[...], gr_ref[...],
                         c_scr, m_scr, reverse)
        out_ref[...] = h.astype(out_ref.dtype)

    run(qk_f, pv_f, nx_f, vt_f, gc_f, gr_f, j, cf_scr, mf_scr, False, hf_out)
    run(qk_b, pv_b, nx_b, vt_b, gc_b, gr_b, nc - 1 - j, cb_scr, mb_scr, True, hb_out)


def _mlstm(mqk, mvt, gcol, grow, mqk_m, mvt_m, grow_m, conv_w, conv_b):
    b, s, _ = mqk.shape
    nc = s // LCHUNK
    hpc = LCHUNK // HALO
    nhb = s // HALO
    cps = mvt.shape[2] // LCHUNK
    spb = mvt.shape[0] // b

    def specs(chunk_of):
        return [
            pl.BlockSpec((None, LCHUNK, 2 * M_WIDTH), lambda bi, j: (bi, chunk_of(j), 0)),
            pl.BlockSpec((None, HALO, 2 * M_WIDTH), lambda bi, j: (bi, jnp.maximum(chunk_of(j) * hpc - 1, 0), 0)),
            pl.BlockSpec((None, HALO, 2 * M_WIDTH),
                         lambda bi, j: (bi, jnp.minimum((chunk_of(j) + 1) * hpc, nhb - 1), 0)),
            pl.BlockSpec((None, M_WIDTH, LCHUNK), lambda bi, j: (bi * spb + chunk_of(j) // cps, 0, chunk_of(j) % cps)),
            pl.BlockSpec((None, LCHUNK, 2 * LANES), lambda bi, j: (bi, chunk_of(j), 0)),
            pl.BlockSpec((4 * M_HEADS, LCHUNK), lambda bi, j: (0, bi * nc + chunk_of(j))),
        ]

    fwd = lambda j: j
    bwd = lambda j: nc - 1 - j
    operands = [mqk, mqk, mqk, mvt, gcol, grow]
    out_spec = lambda chunk_of: pl.BlockSpec((None, LCHUNK, M_WIDTH), lambda bi, j: (bi, chunk_of(j), 0))
    state = [pltpu.VMEM((M_HEADS, CROWS, M_HEAD_DIM), F32), pltpu.VMEM((2 * M_HEADS, LANES), F32)]
    return pl.pallas_call(
        _mlstm_kernel,
        grid=(b, nc),
        in_specs=specs(fwd) + specs(bwd) + [_const_spec(a.shape) for a in (mqk_m, mvt_m, grow_m, conv_w, conv_b)],
        out_specs=[out_spec(fwd), out_spec(bwd)],
        out_shape=[jax.ShapeDtypeStruct((b, s, M_WIDTH), BF16)] * 2,
        scratch_shapes=state + state,
        compiler_params=pltpu.CompilerParams(
            dimension_semantics=("parallel", "arbitrary"), vmem_limit_bytes=VMEM_LIMIT),
        name="mlstm",
    )(*operands, *operands, mqk_m, mvt_m, grow_m, conv_w, conv_b)


def _out_kernel(x_ref, a_ref, hf_ref, hb_ref, mo_ref, sga_ref, sgb_ref,
                mg_ref, wpa_ref, wpb_ref, wo_ref, n2_ref, wg_ref, wu_ref, wd_ref, fn_ref, o_ref):
    h = (hf_ref[...].astype(F32) + hb_ref[...].astype(F32)) * _sigmoid(mo_ref[...].astype(F32))
    mg = mg_ref[...]
    parts = []
    for hd in range(M_HEADS):
        sl = slice(hd * M_HEAD_DIM, (hd + 1) * M_HEAD_DIM)
        parts.append(_rms(h[:, sl], mg[:, sl]).astype(BF16))
    m_out = jnp.concatenate(parts, axis=1)
    merged = (sga_ref[...].astype(F32) * _dot(a_ref[...], wpa_ref[...])
              + sgb_ref[...].astype(F32) * _dot(m_out, wpb_ref[...]))
    y = x_ref[...] + _dot(merged.astype(BF16), wo_ref[...])

    xn = _rms(y, n2_ref[...]).astype(BF16)
    d_ff = wg_ref.shape[1]
    for c0 in range(0, d_ff, FF_CHUNK):
        c1 = min(c0 + FF_CHUNK, d_ff)
        g = _dot(xn, wg_ref[:, c0:c1])
        u = _dot(xn, wu_ref[:, c0:c1])
        y = y + _dot((g * _sigmoid(g) * u).astype(BF16), wd_ref[c0:c1, :])
    o_ref[...] = _rms(y, fn_ref[...])


def _out_proj(x2, a2, hf2, hb2, mo2, sga2, sgb2, wts, tb):
    n, d = x2.shape
    row = lambda w: pl.BlockSpec((tb, w), lambda i: (i, 0))
    const = lambda w: pl.BlockSpec(w.shape, lambda i: (0,) * w.ndim, pipeline_mode=pl.Buffered(1))
    return pl.pallas_call(
        _out_kernel,
        grid=(n // tb,),
        in_specs=[row(d), row(a2.shape[1]), row(M_WIDTH), row(M_WIDTH), row(M_WIDTH), row(d), row(d)]
        + [const(w) for w in wts],
        out_specs=row(d),
        out_shape=jax.ShapeDtypeStruct((n, d), F32),
        compiler_params=pltpu.CompilerParams(dimension_semantics=("parallel",), vmem_limit_bytes=VMEM_LIMIT),
        name="out_proj",
    )(x2, a2, hf2, hb2, mo2, sga2, sgb2, *wts)


def _rope_tables(pos):
    half = QK_ROPE // 2
    n = pos.shape[0]
    freqs = ROPE_THETA ** (-jnp.arange(half, dtype=F32) / half)
    per_row = LANES // half
    ang = jnp.repeat(pos.astype(F32).reshape(n // per_row, per_row), half, axis=1) * jnp.tile(freqs, per_row)[None, :]
    c, s = jnp.cos(ang).reshape(n, half), jnp.sin(ang).reshape(n, half)
    lead = jnp.zeros((n, QK_NOPE), F32)
    tail = jnp.zeros((n, LANES - QK_DIM), F32)
    return jnp.concatenate([lead, c, c, tail], axis=1), jnp.concatenate([lead, s, s, tail], axis=1)


def _rot(w):
    half = QK_ROPE // 2
    return jnp.concatenate([-w[..., half:], w[..., :half]], axis=-1)


def _pad_cols(w, left, total):
    return jnp.pad(w, [(0, 0)] * (w.ndim - 1) + [(left, total - left - w.shape[-1])])


def _prep_weights(norm1_g, w_in, b_in, q_norm_g, kv_norm_g, w_uq, w_ukv):
    d = w_in.shape[0]
    o_kr = Q_LORA + KV_LORA
    o_m = o_kr + QK_ROPE
    o_g = o_m + 4 * M_WIDTH
    o_ga = o_g + 4 * M_HEADS

    def seg1(w):
        kr = w[..., o_kr:o_m]
        lead = jnp.zeros(w.shape[:-1] + (QK_NOPE,), w.dtype)
        return jnp.concatenate([w[..., :o_kr], lead, kr, kr, lead, _rot(kr), _rot(kr)], axis=-1)

    def gates8(w):
        gates = w[..., o_g:o_ga].reshape(w.shape[:-1] + (4, M_HEADS))
        return (jnp.concatenate([gates[..., 0, :], gates[..., 2, :]], axis=-1),
                jnp.concatenate([gates[..., 1, :], gates[..., 3, :]], axis=-1))

    def segm(w):
        g_in, g_fg = gates8(w)
        return jnp.concatenate([w[..., o_m:o_m + 2 * M_WIDTH], w[..., o_m + 3 * M_WIDTH:o_g],
                                _pad_cols(g_in, 0, LANES), _pad_cols(g_fg, 0, LANES)], axis=-1)

    def segv(w):
        return jnp.concatenate([w[..., o_m + 2 * M_WIDTH:o_m + 3 * M_WIDTH], *gates8(w)], axis=-1)

    b2 = b_in[None, :]
    w1, b1 = seg1(w_in).astype(BF16), seg1(b2)
    wm, bm = segm(w_in).astype(BF16), segm(b2)
    wg, bg = w_in[:, o_ga:].astype(BF16), b2[:, o_ga:]

    uq = w_uq.reshape(Q_LORA, MLA_HEADS, QK_DIM)
    wuq = jnp.concatenate([uq, _rot(uq[..., QK_NOPE:])], axis=-1).reshape(Q_LORA, MLA_HEADS * LANES).astype(BF16)
    ukv = w_ukv.reshape(KV_LORA, MLA_HEADS, QK_NOPE + V_HEAD)
    wuk = _pad_cols(ukv[..., :QK_NOPE], 0, LANES).reshape(KV_LORA, MLA_HEADS * LANES).astype(BF16)
    wuv = _pad_cols(ukv[..., QK_NOPE:], 0, VROWS).reshape(KV_LORA, MLA_HEADS * VROWS).T.astype(BF16)
    vone = jnp.tile((jnp.arange(VROWS) == V_HEAD).astype(F32), MLA_HEADS)[:, None]
    wmv, bmv = segv(w_in).T.astype(BF16), segv(b2).T
    return (norm1_g[None, :], w1, b1, wm, bm, wg, bg, q_norm_g[None, :], kv_norm_g[None, :], wuq, wuk, wuv, vone,
            wmv, bmv)


def _pick(n, pref):
    return pref if n % pref == 0 else n


def _attn_tq(s):
    return _pick(s, max(512, min(1024, ATTN_STEP_SCORES // s)))


def _trunk(x, rope, meta_parts, in_wts, conv_w, conv_b, merge_wts, ffn_wts):
    b, s, d = x.shape
    n = b * s
    tb = _pick(s, 512)
    k_m, vt_m, mqk_m, mvt_m, gr_m = meta_parts

    x2 = x.reshape(n, d)
    q, k, vt, mqk, mvt, mo, gt, sga, sgb, gr = _in_proj(x2, rope, in_wts, tb, s)

    r3 = lambda a: a.reshape(b, s, a.shape[-1])
    bias = jnp.where(jnp.arange(MCHUNK) >= META_PAD, 0.0, NEG).astype(F32)[:, None]
    a_out = _attention(r3(q), k.reshape(n // tb, tb, k.shape[-1]), vt, k_m, vt_m, bias, _attn_tq(s))

    h_f, h_b = _mlstm(r3(mqk), mvt, r3(gt), gr, mqk_m, mvt_m, gr_m, conv_w, conv_b)

    y = _out_proj(x2, a_out.reshape(n, -1), h_f.reshape(n, -1), h_b.reshape(n, -1), mo, sga, sgb,
                  merge_wts + ffn_wts, tb)
    return y.reshape(b, s, d)


def kernel(x_prompt, x_sample, meta_tokens, norm1_g, w_in, b_in, conv_w, conv_b, q_norm_g, kv_norm_g, w_uq, w_ukv, m_norm_g, w_pa, w_pb, w_o, norm2_g, w_ffn_gate, w_ffn_up, w_ffn_down, final_norm_g):
    assert w_in.shape[0] == 1, "single-layer trunk"
    d = x_prompt.shape[-1]
    in_wts = _prep_weights(norm1_g[0], w_in[0], b_in[0], q_norm_g[0], kv_norm_g[0], w_uq[0], w_ukv[0])
    merge_wts = (m_norm_g[0][None, :], w_pa[0].astype(BF16), w_pb[0].astype(BF16), w_o[0].astype(BF16))
    ffn_wts = (norm2_g[0][None, :], w_ffn_gate[0].astype(BF16), w_ffn_up[0].astype(BF16),
               w_ffn_down[0].astype(BF16), final_norm_g[None, :])
    cw, cb = conv_w[0], conv_b[0][None, :]

    hm = jnp.concatenate([jnp.zeros((META_PAD, d), x_prompt.dtype), meta_tokens.astype(x_prompt.dtype)], axis=0)
    rope_m = _rope_tables(jnp.arange(MCHUNK) - META_PAD)
    _, k_m, vt_m, mqk_m, mvt_m, _, _, _, _, gr_m = _in_proj(hm, rope_m, in_wts, MCHUNK, MCHUNK)
    meta_parts = (k_m, vt_m[0], mqk_m, mvt_m[0], gr_m)

    rope = _rope_tables(N_META + jnp.arange(max(x_prompt.shape[1], x_sample.shape[1])))
    y_prompt = _trunk(x_prompt, rope, meta_parts, in_wts, cw, cb, merge_wts, ffn_wts)
    y_sample = _trunk(x_sample, rope, meta_parts, in_wts, cw, cb, merge_wts, ffn_wts)
    return (y_prompt, y_sample)
```

```python
import functools

import jax
import jax.numpy as jnp
from jax import lax
from jax.experimental import pallas as pl
from jax.experimental.pallas import tpu as pltpu

F32 = jnp.float32
BF16 = jnp.bfloat16

EPS = 1e-6
N_META = 16
ROPE_THETA = 10000.0
MLA_HEADS = 8
QK_NOPE = 64
QK_ROPE = 32
V_HEAD = 64
QK_DIM = QK_NOPE + QK_ROPE
Q_LORA = 256
KV_LORA = 128
M_HEADS = 4
M_HEAD_DIM = 128
M_WIDTH = M_HEADS * M_HEAD_DIM
LANES = 128
MCHUNK = 128
LCHUNK = 256
META_PAD = MCHUNK - N_META
HALO = 16
NEG = -1e30
LOG2E = 1.4426950408889634
GROWTH_LIMIT = 64.0
ATTN_STEP_SCORES = 512 * 16384
FAST_UNROLL = 16
VMEM_LIMIT = 56 * 1024 * 1024


def _dot(a, b):
    return jnp.dot(a, b, preferred_element_type=F32)


def _dot_nt(a, b):
    return lax.dot_general(a, b, (((1,), (1,)), ((), ())), preferred_element_type=F32)


def _dot_tn(a, b):
    return lax.dot_general(a, b, (((0,), (0,)), ((), ())), preferred_element_type=F32)


def _sigmoid(x):
    return 1.0 / (1.0 + jnp.exp(-x))


def _rms(x, g):
    return x * lax.rsqrt(jnp.mean(x * x, axis=-1, keepdims=True) + EPS) * g


def _const_spec(shape):
    return pl.BlockSpec(shape, lambda *_: (0,) * len(shape))


def _in_proj_kernel(x_ref, cos_ref, sin_ref, n1_ref, w1_ref, b1_ref, wm_ref, bm_ref, wg_ref, bg_ref,
                    qn_ref, kvn_ref, wuq_ref, wuk_ref, wuv_ref, vone_ref, wmv_ref, bmv_ref,
                    q_out, k_out, v_out, mqk_out, mv_out, mo_out, gt_out, sga_out, sgb_out, gr_out):
    xn = _rms(x_ref[...], n1_ref[...]).astype(BF16)
    cos, sin = cos_ref[...], sin_ref[...]
    cos_hi, sin_hi = pltpu.roll(cos, QK_ROPE, axis=1), pltpu.roll(sin, QK_ROPE, axis=1)
    lane = lax.broadcasted_iota(jnp.int32, cos.shape, 1)
    tq = jnp.where(lane < QK_NOPE, 1.0, cos + sin_hi)

    p1 = _dot(xn, w1_ref[...]) + b1_ref[...]
    c_q = p1[:, :Q_LORA]
    c_kv = p1[:, Q_LORA:Q_LORA + KV_LORA]
    kr_plain = p1[:, Q_LORA + KV_LORA:Q_LORA + KV_LORA + LANES]
    kr_rot = p1[:, Q_LORA + KV_LORA + LANES:]
    k_rope = kr_plain * (cos + cos_hi) + kr_rot * (sin + sin_hi)

    cqn = (_rms(c_q, qn_ref[...]) * (QK_DIM ** -0.5 * LOG2E)).astype(BF16)
    qa = _dot(cqn, wuq_ref[...])
    ckvn = _rms(c_kv, kvn_ref[...]).astype(BF16)
    ka = _dot(ckvn, wuk_ref[...])
    for h in range(MLA_HEADS):
        sl = slice(h * LANES, (h + 1) * LANES)
        q_out[:, sl] = (qa[:, sl] * tq).astype(BF16)
        k_out[:, sl] = (ka[:, sl] + k_rope).astype(BF16)
    v_out[...] = (_dot_nt(wuv_ref[...], ckvn) + vone_ref[...]).astype(BF16)

    pm = _dot(xn, wm_ref[...]) + bm_ref[...]
    mqk_out[...] = pm[:, :2 * M_WIDTH].astype(BF16)
    mo_out[...] = pm[:, 2 * M_WIDTH:3 * M_WIDTH].astype(BF16)
    gt_out[...] = pm[:, 3 * M_WIDTH:]
    mt = _dot_nt(wmv_ref[...], xn) + bmv_ref[...]
    mv_out[...] = mt[:M_WIDTH, :].astype(BF16)
    gr_out[...] = mt[M_WIDTH:, :]

    pg = _dot(xn, wg_ref[...]) + bg_ref[...]
    d = sga_out.shape[-1]
    sga_out[...] = _sigmoid(pg[:, :d]).astype(BF16)
    sgb_out[...] = _sigmoid(pg[:, d:]).astype(BF16)


def _in_proj(x2, rope_tabs, wts, tb, seq):
    n, d = x2.shape
    rope = pl.BlockSpec((tb, LANES), lambda i: (i % (seq // tb), 0))
    widths = (MLA_HEADS * LANES, MLA_HEADS * LANES, MLA_HEADS * VROWS, 2 * M_WIDTH, M_WIDTH, M_WIDTH, 2 * LANES, d, d)
    dtypes = (BF16, BF16, BF16, BF16, BF16, BF16, F32, BF16, BF16)
    row = lambda w: pl.BlockSpec((tb, w), lambda i: (i, 0))
    out_specs = [row(w) for w in widths]
    out_shape = [jax.ShapeDtypeStruct((n, w), t) for w, t in zip(widths, dtypes)]
    for o in (2, 4):
        out_specs[o] = pl.BlockSpec((None, widths[o], tb), lambda i: (i, 0, 0))
        out_shape[o] = jax.ShapeDtypeStruct((n // tb, widths[o], tb), BF16)
    out_specs.append(pl.BlockSpec((4 * M_HEADS, tb), lambda i: (0, i)))
    out_shape.append(jax.ShapeDtypeStruct((4 * M_HEADS, n), F32))
    return pl.pallas_call(
        _in_proj_kernel,
        grid=(n // tb,),
        in_specs=[row(d)] + [rope] * len(rope_tabs) + [_const_spec(w.shape) for w in wts],
        out_specs=out_specs,
        out_shape=out_shape,
        compiler_params=pltpu.CompilerParams(dimension_semantics=("parallel",), vmem_limit_bytes=VMEM_LIMIT),
        name="in_proj",
    )(x2, *rope_tabs, *wts)


def _attn_kernel(q_ref, k_ref, vt_ref, km_ref, vtm_ref, bias_ref, o_ref,
                 m_scr, acc_scr, g_scr, mx_scr, al_scr, s_scr, p_scr, *, nk):
    hsl = [slice(0, LANES), slice(LANES, 2 * LANES)]
    vsl = [slice(0, VROWS), slice(VROWS, 2 * VROWS)]
    qs = [q_ref[:, sl] for sl in hsl]

    def meta_init():
        for hh, sl in enumerate(hsl):
            s = _dot_nt(km_ref[:, sl], qs[hh]) + bias_ref[...]
            m = jnp.max(s, axis=0, keepdims=True)
            p = jnp.exp2(s - m).astype(BF16)
            m_scr[hh] = m
            acc_scr[hh] = _dot(vtm_ref[vsl[hh], :], p)

    def pipeline(first, step, n_steps, last, unroll):
        first()

        def body(t, carry):
            for u in range(unroll):
                step(unroll * t + u, u % 2)
            return carry

        n_loop = n_steps // unroll
        lax.fori_loop(0, n_loop, body, 0)
        for i in range(n_loop * unroll, n_steps):
            step(i, i % 2)
        last()

    def scores_exp(c, slot):
        for hh, sl in enumerate(hsl):
            s = _dot_nt(k_ref[c, :, sl], qs[hh])
            ref = m_scr[hh]
            p_scr[slot, hh] = jnp.exp2(s - ref).astype(BF16)
            m_new = jnp.maximum(ref, jnp.max(s, axis=0, keepdims=True))
            al_scr[slot, hh] = jnp.exp2(ref - m_new)
            g_scr[hh] = jnp.maximum(g_scr[hh], m_new - ref)
            m_scr[hh] = m_new

    def values_rebase(c, slot):
        for hh, sl in enumerate(hsl):
            acc_scr[hh] = (acc_scr[hh] + _dot(vt_ref[c, vsl[hh], :], p_scr[slot, hh])) * al_scr[slot, hh]

    def fast_step(i, par):
        scores_exp(i + 1, 1 - par)
        values_rebase(i, par)

    meta_init()
    g_scr[...] = jnp.zeros_like(g_scr)
    def finalize():
        o_t = [acc_scr[hh][:V_HEAD, :] / acc_scr[hh][V_HEAD:V_HEAD + 1, :] for hh in range(2)]
        o_ref[...] = jnp.concatenate(o_t, axis=0).T.astype(BF16)

    pipeline(lambda: scores_exp(0, 0), fast_step, nk - 1, lambda: values_rebase(nk - 1, (nk - 1) % 2), FAST_UNROLL)
    finalize()

    def scores(c, slot):
        for hh, sl in enumerate(hsl):
            s = _dot_nt(k_ref[c, :, sl], qs[hh])
            mx_scr[slot, hh] = jnp.max(s, axis=0, keepdims=True)
            s_scr[slot, hh] = s

    def softmax(slot):
        for hh in range(2):
            m_prev = m_scr[hh]
            m_new = jnp.maximum(m_prev, mx_scr[slot, hh])
            al_scr[slot, hh] = jnp.exp2(m_prev - m_new)
            m_scr[hh] = m_new
            p_scr[slot, hh] = jnp.exp2(s_scr[slot, hh] - m_new).astype(BF16)

    def values(c, slot):
        for hh, sl in enumerate(hsl):
            acc_scr[hh] = al_scr[slot, hh] * acc_scr[hh] + _dot(vt_ref[c, vsl[hh], :], p_scr[slot, hh])

    def safe_first():
        scores(0, 0)
        scores(1, 1)
        softmax(0)

    def safe_step(i, par):
        scores(i + 2, par)
        softmax(1 - par)
        values(i, par)

    def safe_last():
        softmax(1)
        values(nk - 2, 0)
        values(nk - 1, 1)

    @pl.when(jnp.max(jnp.maximum(g_scr[0], g_scr[1])) > GROWTH_LIMIT)
    def _():
        meta_init()
        pipeline(safe_first, safe_step, nk - 2, safe_last, 2)
        finalize()


def _attention(q, k, vt, km, vtm, bias, tq):
    b, s, _ = q.shape
    nk = k.shape[0] // b
    assert nk >= 2 and nk % 2 == 0, "the key-chunk pipeline is unrolled by two"
    tk = k.shape[1]
    hp = MLA_HEADS // 2
    w2 = 2 * LANES
    return pl.pallas_call(
        functools.partial(_attn_kernel, nk=nk),
        grid=(b, hp, s // tq),
        in_specs=[
            pl.BlockSpec((None, tq, w2), lambda bi, h, i: (bi, i, h)),
            pl.BlockSpec((nk, tk, w2), lambda bi, h, i: (bi, 0, h)),
            pl.BlockSpec((nk, 2 * VROWS, tk), lambda bi, h, i: (bi, h, 0)),
            pl.BlockSpec((MCHUNK, w2), lambda bi, h, i: (0, h)),
            pl.BlockSpec((2 * VROWS, MCHUNK), lambda bi, h, i: (h, 0)),
            pl.BlockSpec((MCHUNK, 1), lambda bi, h, i: (0, 0)),
        ],
        out_specs=pl.BlockSpec((None, tq, LANES), lambda bi, h, i: (bi, i, h)),
        out_shape=jax.ShapeDtypeStruct((b, s, hp * LANES), BF16),
        scratch_shapes=[
            pltpu.VMEM((2, 1, tq), F32), pltpu.VMEM((2, VROWS, tq), F32),
            pltpu.VMEM((2, 1, tq), F32),
            pltpu.VMEM((2, 2, 1, tq), F32), pltpu.VMEM((2, 2, 1, tq), F32),
            pltpu.VMEM((2, 2, tk, tq), F32), pltpu.VMEM((2, 2, tk, tq), BF16),
        ],
        compiler_params=pltpu.CompilerParams(
            dimension_semantics=("parallel", "parallel", "arbitrary"), vmem_limit_bytes=VMEM_LIMIT),
        name="mla_attention",
    )(q, k, vt, km, vtm, bias)


def _conv_silu(cur, prev_row, next_row, cw, cb):
    n = cur.shape[0]
    c0, c1, c2 = cw[0:1, :], cw[1:2, :], cw[2:3, :]
    y = c0 * pltpu.roll(cur, 1, axis=0) + c1 * cur + c2 * pltpu.roll(cur, n - 1, axis=0) + cb
    r8 = lax.broadcasted_iota(jnp.int32, (8, cur.shape[1]), 0)
    top = y[0:8, :] + jnp.where(r8 == 0, c0 * (prev_row - cur[n - 1:n, :]), 0.0)
    bot = y[n - 8:n, :] + jnp.where(r8 == 7, c2 * (next_row - cur[0:1, :]), 0.0)
    y = jnp.concatenate([top, y[8:n - 8, :], bot], axis=0)
    return y * _sigmoid(y)


def _log_sigmoid(x):
    return jnp.minimum(x, 0.0) - jnp.log(1.0 + jnp.exp(-jnp.abs(x)))


def _tri(n, upper):
    r = lax.broadcasted_iota(jnp.int32, (n, n), 0)
    c = lax.broadcasted_iota(jnp.int32, (n, n), 1)
    return (c >= r) if upper else (c <= r)


FF_CHUNK = 1024
VROWS = LANES
AUG = 16
CROWS = M_HEAD_DIM + AUG


def _v_aug_t(vt_h):
    r = lax.broadcasted_iota(jnp.int32, (AUG, vt_h.shape[1]), 0)
    return jnp.concatenate([vt_h, jnp.where(r == 0, 1.0, 0.0).astype(vt_h.dtype)], axis=0)


def _split3(a):
    hi = a.astype(BF16)
    r = a - hi.astype(F32)
    mid = r.astype(BF16)
    return hi, mid, (r - mid.astype(F32)).astype(BF16)


def _gate_rows(gi8, gf8, valid, tri_t, last, m_prev8):
    a8 = _log_sigmoid(gf8)
    b8 = gi8
    if valid is not None:
        a8 = jnp.where(valid, a8, 0.0)
        b8 = jnp.where(valid, b8, NEG)
    f8 = sum(_dot(part, tri_t) for part in _split3(a8))
    f_tot = f8[:, last:last + 1]
    g8 = f_tot - f8 + b8
    m_new8 = jnp.maximum(f_tot + m_prev8, jnp.max(g8, axis=1, keepdims=True))
    return f8, jnp.exp(g8 - m_new8), jnp.exp(f_tot + m_prev8 - m_new8), m_new8


def _state_update(k_h, vaug_t, w_row, decay, ct_ref, h):
    vw = (vaug_t.astype(F32) * w_row).astype(BF16)
    ct_ref[h] = decay * ct_ref[h] + _dot(vw, k_h)


def _mlstm_chunk(q, k, vt_ref, gc, gr, ct_ref, m_ref, reverse):
    n = q.shape[0]
    r0 = M_HEADS if reverse else 0
    mask_t = _tri(n, not reverse)
    tri_t = mask_t.astype(BF16)
    tri_c = _tri(n, reverse).astype(BF16)
    last = 0 if reverse else n - 1
    m_prev8 = m_ref[:, 0:1]
    f8, w8, decay8, m_new8 = _gate_rows(gr[0:8, :], gr[8:16, :], None, tri_t, last, m_prev8)
    inter8 = f8 + m_prev8
    c_cols = sum(_dot(tri_c, part) for part in _split3(_log_sigmoid(gc[:, LANES:]))) - gc[:, :LANES]
    outs = []
    for h in range(M_HEADS):
        r = r0 + h
        sl = slice(h * M_HEAD_DIM, (h + 1) * M_HEAD_DIM)
        q_h = q[:, sl].astype(BF16)
        k_h = k[:, sl].astype(BF16)
        vaug_t = _v_aug_t(vt_ref[sl, :])
        inter = inter8[r:r + 1, :]
        log_d = jnp.where(mask_t, f8[r:r + 1, :] - c_cols[:, r:r + 1], NEG)
        m_j = jnp.maximum(inter, jnp.max(log_d, axis=0, keepdims=True))
        s_t = (_dot_nt(k_h, q_h) * jnp.exp(log_d - m_j)).astype(BF16)
        h_t = _dot_nt(ct_ref[h].astype(BF16), q_h) * jnp.exp(inter - m_j) + _dot(vaug_t, s_t)
        den = h_t[M_HEAD_DIM:M_HEAD_DIM + 1, :]
        out_t = h_t[:M_HEAD_DIM, :] / jnp.maximum(jnp.abs(den), jnp.exp(-m_j))
        outs.append(out_t.T)
        _state_update(k_h, vaug_t, w8[r:r + 1, :], decay8[r:r + 1, :], ct_ref, h)
    m_ref[...] = jnp.broadcast_to(m_new8, m_ref.shape)
    return jnp.concatenate(outs, axis=1)


def _mlstm_kernel(qk_f, pv_f, nx_f, vt_f, gc_f, gr_f, qk_b, pv_b, nx_b, vt_b, gc_b, gr_b,
                  qk_m, vt_m, gr_m, cw_ref, cb_ref, hf_out, hb_out, cf_scr, mf_scr, cb_scr, mb_scr):
    j = pl.program_id(1)
    nc = pl.num_programs(1)
    cw = cw_ref[...]
    cb = cb_ref[...]
    scale = M_HEAD_DIM ** -0.5
    meta_last = qk_m[MCHUNK - 1:MCHUNK, :].astype(F32)

    @pl.when(j == 0)
    def _():
        cf_scr[...] = jnp.zeros_like(cf_scr)
        cb_scr[...] = jnp.zeros_like(cb_scr)
        mb_scr[...] = jnp.zeros_like(mb_scr)
        valid_col = lax.broadcasted_iota(jnp.int32, (MCHUNK, 1), 0) >= META_PAD
        valid_row = lax.broadcasted_iota(jnp.int32, (1, MCHUNK), 1) >= META_PAD
        cur = jnp.where(valid_col, qk_m[...].astype(F32), 0.0)
        first_x = qk_f[0:1, :].astype(F32)
        k_all = _conv_silu(cur, jnp.zeros_like(first_x), first_x, cw, cb)[:, M_WIDTH:] * scale
        _, w8, decay8, m_new8 = _gate_rows(gr_m[0:8, :], gr_m[8:16, :], valid_row, _tri(MCHUNK, True).astype(BF16),
                                           MCHUNK - 1, jnp.zeros((8, 1), F32))
        for h in range(M_HEADS):
            sl = slice(h * M_HEAD_DIM, (h + 1) * M_HEAD_DIM)
            _state_update(k_all[:, sl].astype(BF16), _v_aug_t(vt_m[sl, :]), w8[h:h + 1, :], decay8[h:h + 1, :],
                          cf_scr, h)
        mf_scr[...] = jnp.broadcast_to(m_new8, mf_scr.shape)

    def run(qk_ref, pv_ref, nx_ref, vt_ref, gc_ref, gr_ref, c, c_scr, m_scr, reverse, out_ref):
        prev_row = jnp.where(c == 0, meta_last, pv_ref[HALO - 1:HALO, :].astype(F32))
        next_row = jnp.where(c == nc - 1, 0.0, nx_ref[0:1, :].astype(F32))
        qk = _conv_silu(qk_ref[...].astype(F32), prev_row, next_row, cw, cb)
        h = _mlstm_chunk(qk[:, :M_WIDTH], qk[:, M_WIDTH:] * scale, vt_ref, gc_ref[...], gr_ref[...],
                         c_scr, m_scr, reverse)
        out_ref[...] = h.astype(out_ref.dtype)

    run(qk_f, pv_f, nx_f, vt_f, gc_f, gr_f, j, cf_scr, mf_scr, False, hf_out)
    run(qk_b, pv_b, nx_b, vt_b, gc_b, gr_b, nc - 1 - j, cb_scr, mb_scr, True, hb_out)


def _mlstm(mqk, mvt, gcol, grow, mqk_m, mvt_m, grow_m, conv_w, conv_b):
    b, s, _ = mqk.shape
    nc = s // LCHUNK
    hpc = LCHUNK // HALO
    nhb = s // HALO
    cps = mvt.shape[2] // LCHUNK
    spb = mvt.shape[0] // b

    def specs(chunk_of):
        return [
            pl.BlockSpec((None, LCHUNK, 2 * M_WIDTH), lambda bi, j: (bi, chunk_of(j), 0)),
            pl.BlockSpec((None, HALO, 2 * M_WIDTH), lambda bi, j: (bi, jnp.maximum(chunk_of(j) * hpc - 1, 0), 0)),
            pl.BlockSpec((None, HALO, 2 * M_WIDTH),
                         lambda bi, j: (bi, jnp.minimum((chunk_of(j) + 1) * hpc, nhb - 1), 0)),
            pl.BlockSpec((None, M_WIDTH, LCHUNK), lambda bi, j: (bi * spb + chunk_of(j) // cps, 0, chunk_of(j) % cps)),
            pl.BlockSpec((None, LCHUNK, 2 * LANES), lambda bi, j: (bi, chunk_of(j), 0)),
            pl.BlockSpec((4 * M_HEADS, LCHUNK), lambda bi, j: (0, bi * nc + chunk_of(j))),
        ]

    fwd = lambda j: j
    bwd = lambda j: nc - 1 - j
    operands = [mqk, mqk, mqk, mvt, gcol, grow]
    out_spec = lambda chunk_of: pl.BlockSpec((None, LCHUNK, M_WIDTH), lambda bi, j: (bi, chunk_of(j), 0))
    state = [pltpu.VMEM((M_HEADS, CROWS, M_HEAD_DIM), F32), pltpu.VMEM((2 * M_HEADS, LANES), F32)]
    return pl.pallas_call(
        _mlstm_kernel,
        grid=(b, nc),
        in_specs=specs(fwd) + specs(bwd) + [_const_spec(a.shape) for a in (mqk_m, mvt_m, grow_m, conv_w, conv_b)],
        out_specs=[out_spec(fwd), out_spec(bwd)],
        out_shape=[jax.ShapeDtypeStruct((b, s, M_WIDTH), BF16)] * 2,
        scratch_shapes=state + state,
        compiler_params=pltpu.CompilerParams(
            dimension_semantics=("parallel", "arbitrary"), vmem_limit_bytes=VMEM_LIMIT),
        name="mlstm",
    )(*operands, *operands, mqk_m, mvt_m, grow_m, conv_w, conv_b)


def _out_kernel(x_ref, a_ref, hf_ref, hb_ref, mo_ref, sga_ref, sgb_ref,
                mg_ref, wpa_ref, wpb_ref, wo_ref, n2_ref, wg_ref, wu_ref, wd_ref, fn_ref, o_ref):
    h = (hf_ref[...].astype(F32) + hb_ref[...].astype(F32)) * _sigmoid(mo_ref[...].astype(F32))
    mg = mg_ref[...]
    parts = []
    for hd in range(M_HEADS):
        sl = slice(hd * M_HEAD_DIM, (hd + 1) * M_HEAD_DIM)
        parts.append(_rms(h[:, sl], mg[:, sl]).astype(BF16))
    m_out = jnp.concatenate(parts, axis=1)
    merged = (sga_ref[...].astype(F32) * _dot(a_ref[...], wpa_ref[...])
              + sgb_ref[...].astype(F32) * _dot(m_out, wpb_ref[...]))
    y = x_ref[...] + _dot(merged.astype(BF16), wo_ref[...])

    xn = _rms(y, n2_ref[...]).astype(BF16)
    d_ff = wg_ref.shape[1]
    for c0 in range(0, d_ff, FF_CHUNK):
        c1 = min(c0 + FF_CHUNK, d_ff)
        g = _dot(xn, wg_ref[:, c0:c1])
        u = _dot(xn, wu_ref[:, c0:c1])
        y = y + _dot((g * _sigmoid(g) * u).astype(BF16), wd_ref[c0:c1, :])
    o_ref[...] = _rms(y, fn_ref[...])


def _out_proj(x2, a2, hf2, hb2, mo2, sga2, sgb2, wts, tb):
    n, d = x2.shape
    row = lambda w: pl.BlockSpec((tb, w), lambda i: (i, 0))
    const = lambda w: pl.BlockSpec(w.shape, lambda i: (0,) * w.ndim, pipeline_mode=pl.Buffered(1))
    return pl.pallas_call(
        _out_kernel,
        grid=(n // tb,),
        in_specs=[row(d), row(a2.shape[1]), row(M_WIDTH), row(M_WIDTH), row(M_WIDTH), row(d), row(d)]
        + [const(w) for w in wts],
        out_specs=row(d),
        out_shape=jax.ShapeDtypeStruct((n, d), F32),
        compiler_params=pltpu.CompilerParams(dimension_semantics=("parallel",), vmem_limit_bytes=VMEM_LIMIT),
        name="out_proj",
    )(x2, a2, hf2, hb2, mo2, sga2, sgb2, *wts)


def _rope_tables(pos):
    half = QK_ROPE // 2
    n = pos.shape[0]
    freqs = ROPE_THETA ** (-jnp.arange(half, dtype=F32) / half)
    per_row = LANES // half
    ang = jnp.repeat(pos.astype(F32).reshape(n // per_row, per_row), half, axis=1) * jnp.tile(freqs, per_row)[None, :]
    c, s = jnp.cos(ang).reshape(n, half), jnp.sin(ang).reshape(n, half)
    lead = jnp.zeros((n, QK_NOPE), F32)
    tail = jnp.zeros((n, LANES - QK_DIM), F32)
    return jnp.concatenate([lead, c, c, tail], axis=1), jnp.concatenate([lead, s, s, tail], axis=1)


def _rot(w):
    half = QK_ROPE // 2
    return jnp.concatenate([-w[..., half:], w[..., :half]], axis=-1)


def _pad_cols(w, left, total):
    return jnp.pad(w, [(0, 0)] * (w.ndim - 1) + [(left, total - left - w.shape[-1])])


def _prep_weights(norm1_g, w_in, b_in, q_norm_g, kv_norm_g, w_uq, w_ukv):
    d = w_in.shape[0]
    o_kr = Q_LORA + KV_LORA
    o_m = o_kr + QK_ROPE
    o_g = o_m + 4 * M_WIDTH
    o_ga = o_g + 4 * M_HEADS

    def seg1(w):
        kr = w[..., o_kr:o_m]
        lead = jnp.zeros(w.shape[:-1] + (QK_NOPE,), w.dtype)
        return jnp.concatenate([w[..., :o_kr], lead, kr, kr, lead, _rot(kr), _rot(kr)], axis=-1)

    def gates8(w):
        gates = w[..., o_g:o_ga].reshape(w.shape[:-1] + (4, M_HEADS))
        return (jnp.concatenate([gates[..., 0, :], gates[..., 2, :]], axis=-1),
                jnp.concatenate([gates[..., 1, :], gates[..., 3, :]], axis=-1))

    def segm(w):
        g_in, g_fg = gates8(w)
        return jnp.concatenate([w[..., o_m:o_m + 2 * M_WIDTH], w[..., o_m + 3 * M_WIDTH:o_g],
                                _pad_cols(g_in, 0, LANES), _pad_cols(g_fg, 0, LANES)], axis=-1)

    def segv(w):
        return jnp.concatenate([w[..., o_m + 2 * M_WIDTH:o_m + 3 * M_WIDTH], *gates8(w)], axis=-1)

    b2 = b_in[None, :]
    w1, b1 = seg1(w_in).astype(BF16), seg1(b2)
    wm, bm = segm(w_in).astype(BF16), segm(b2)
    wg, bg = w_in[:, o_ga:].astype(BF16), b2[:, o_ga:]

    uq = w_uq.reshape(Q_LORA, MLA_HEADS, QK_DIM)
    wuq = jnp.concatenate([uq, _rot(uq[..., QK_NOPE:])], axis=-1).reshape(Q_LORA, MLA_HEADS * LANES).astype(BF16)
    ukv = w_ukv.reshape(KV_LORA, MLA_HEADS, QK_NOPE + V_HEAD)
    wuk = _pad_cols(ukv[..., :QK_NOPE], 0, LANES).reshape(KV_LORA, MLA_HEADS * LANES).astype(BF16)
    wuv = _pad_cols(ukv[..., QK_NOPE:], 0, VROWS).reshape(KV_LORA, MLA_HEADS * VROWS).T.astype(BF16)
    vone = jnp.tile((jnp.arange(VROWS) == V_HEAD).astype(F32), MLA_HEADS)[:, None]
    wmv, bmv = segv(w_in).T.astype(BF16), segv(b2).T
    return (norm1_g[None, :], w1, b1, wm, bm, wg, bg, q_norm_g[None, :], kv_norm_g[None, :], wuq, wuk, wuv, vone,
            wmv, bmv)


def _pick(n, pref):
    return pref if n % pref == 0 else n


def _attn_tq(s):
    return _pick(s, max(512, min(1024, ATTN_STEP_SCORES // s)))


def _trunk(x, rope, meta_parts, in_wts, conv_w, conv_b, merge_wts, ffn_wts):
    b, s, d = x.shape
    n = b * s
    tb = _pick(s, 512)
    k_m, vt_m, mqk_m, mvt_m, gr_m = meta_parts

    x2 = x.reshape(n, d)
    q, k, vt, mqk, mvt, mo, gt, sga, sgb, gr = _in_proj(x2, rope, in_wts, tb, s)

    r3 = lambda a: a.reshape(b, s, a.shape[-1])
    bias = jnp.where(jnp.arange(MCHUNK) >= META_PAD, 0.0, NEG).astype(F32)[:, None]
    a_out = _attention(r3(q), k.reshape(n // tb, tb, k.shape[-1]), vt, k_m, vt_m, bias, _attn_tq(s))

    h_f, h_b = _mlstm(r3(mqk), mvt, r3(gt), gr, mqk_m, mvt_m, gr_m, conv_w, conv_b)

    y = _out_proj(x2, a_out.reshape(n, -1), h_f.reshape(n, -1), h_b.reshape(n, -1), mo, sga, sgb,
                  merge_wts + ffn_wts, tb)
    return y.reshape(b, s, d)


def kernel(x_prompt, x_sample, meta_tokens, norm1_g, w_in, b_in, conv_w, conv_b, q_norm_g, kv_norm_g, w_uq, w_ukv, m_norm_g, w_pa, w_pb, w_o, norm2_g, w_ffn_gate, w_ffn_up, w_ffn_down, final_norm_g):
    assert w_in.shape[0] == 1, "single-layer trunk"
    d = x_prompt.shape[-1]
    in_wts = _prep_weights(norm1_g[0], w_in[0], b_in[0], q_norm_g[0], kv_norm_g[0], w_uq[0], w_ukv[0])
    merge_wts = (m_norm_g[0][None, :], w_pa[0].astype(BF16), w_pb[0].astype(BF16), w_o[0].astype(BF16))
    ffn_wts = (norm2_g[0][None, :], w_ffn_gate[0].astype(BF16), w_ffn_up[0].astype(BF16),
               w_ffn_down[0].astype(BF16), final_norm_g[None, :])
    cw, cb = conv_w[0], conv_b[0][None, :]

    hm = jnp.concatenate([jnp.zeros((META_PAD, d), x_prompt.dtype), meta_tokens.astype(x_prompt.dtype)], axis=0)
    rope_m = _rope_tables(jnp.arange(MCHUNK) - META_PAD)
    _, k_m, vt_m, mqk_m, mvt_m, _, _, _, _, gr_m = _in_proj(hm, rope_m, in_wts, MCHUNK, MCHUNK)
    meta_parts = (k_m, vt_m[0], mqk_m, mvt_m[0], gr_m)

    rope = _rope_tables(N_META + jnp.arange(max(x_prompt.shape[1], x_sample.shape[1])))
    y_prompt = _trunk(x_prompt, rope, meta_parts, in_wts, cw, cb, merge_wts, ffn_wts)
    y_sample = _trunk(x_sample, rope, meta_parts, in_wts, cw, cb, merge_wts, ffn_wts)
    return (y_prompt, y_sample)
```
